```python
import math
import jax
import jax.numpy as jnp
from jax import lax
import numpy as np

D_MODEL = 1024
BATCH = 2
SEQ = 8192
DEPTH = 2
DEC_BATCH = 32
DEC_SEQ = 4
PAST_LEN = 8192
PAGE_SIZE = 128

D_MIX = D_MODEL
HEAD_DIM = 64
GROUP_WIDTH = D_MIX // 4
SB_HEADS = GROUP_WIDTH // HEAD_DIM
SB_QBLOCK = 128
SB_BIAS_INIT = -7.0
SSM_HEADS = GROUP_WIDTH // HEAD_DIM
SSM_GROUPS = 2
SSM_STATE = 128
SSM_CONV = 4
SSM_CHUNK = 128
SSM_CONV_DIM = GROUP_WIDTH + 2 * SSM_GROUPS * SSM_STATE
RWKV_HEADS = GROUP_WIDTH // HEAD_DIM
DECAY_LORA = GROUP_WIDTH // 8
ICLR_LORA = GROUP_WIDTH // 8
GATE_LORA = GROUP_WIDTH // 4
RWKV_PROJ = 3 * GROUP_WIDTH + DECAY_LORA + ICLR_LORA + GATE_LORA
RWKV_GN_EPS = HEAD_DIM * 1e-5
SGU_GROUPS = GROUP_WIDTH // HEAD_DIM
SGU_CHUNK = 128
P_A = 3 * GROUP_WIDTH
P_B = GROUP_WIDTH + SSM_CONV_DIM + SSM_HEADS
P_C = RWKV_PROJ
P_D = 2 * GROUP_WIDTH
P_TOTAL = P_A + P_B + P_C + P_D
D_FF = 2816
N_EXPERTS = 8
TOP_K = 2
D_FF_EXPERT = 3584
MOE_BLOCK = 128
N_DENSE = (DEPTH + 1) // 2
N_MOE = DEPTH // 2
RMS_EPS = 1e-6

kernel_name = 'hybrid_sb_ssd_rwkv7_sgu_decoder_step'


def rms_norm(x, gain):
    x32 = x.astype(jnp.float32)
    y = x32 * lax.rsqrt(jnp.mean(x32 * x32, axis=-1, keepdims=True) + RMS_EPS)
    return (y * gain.astype(jnp.float32)).astype(x.dtype)


def sb_query_block(q_blk, q_pos, k, v, k_pos, bias):
    z = jnp.einsum('bqhd,bshd->bhqs', q_blk, k, preferred_element_type=jnp.float32) * (HEAD_DIM ** -0.5)
    z = z + bias.astype(jnp.float32)[None, :, None, None]
    readable = k_pos[None, :] < q_pos[:, None]
    log_beta = jax.nn.log_sigmoid(z)
    log_keep = jnp.where(readable, log_beta - z, 0.0)
    shifted = jnp.concatenate([log_keep[..., 1:], jnp.zeros_like(log_keep[..., :1])], axis=-1)
    tail = lax.cumsum(shifted, axis=3, reverse=True)
    weight = jnp.where(readable, jnp.exp(log_beta + tail), 0.0)
    return jnp.einsum('bhqs,bshd->bqhd', weight, v.astype(jnp.float32))


def stick_breaking(q, k, v, n_past, bias):
    b, L, H, dh = q.shape
    qb = SB_QBLOCK if L % SB_QBLOCK == 0 else L
    nb = L // qb
    k_pos = jnp.arange(k.shape[1])
    q_pos = (n_past + jnp.arange(L)).reshape(nb, qb)
    q_blocks = jnp.moveaxis(q.reshape(b, nb, qb, H, dh), 1, 0)
    out = lax.map(lambda a: sb_query_block(a[0], a[1], k, v, k_pos, bias), (q_blocks, q_pos))
    return jnp.moveaxis(out, 0, 1).reshape(b, L, H, dh)


def mixer_stick_breaking(p, lp, k_past, v_past):
    b, L, _ = p.shape
    q, k, v = (t.reshape(b, L, SB_HEADS, HEAD_DIM) for t in jnp.split(p, 3, axis=-1))
    q = rms_norm(q, lp['q_norm'])
    k = rms_norm(k, lp['k_norm'])
    if k_past is None:
        k_all, v_all, n_past = k, v, 0
    else:
        k_all = jnp.concatenate([k_past.astype(k.dtype), k], axis=1)
        v_all = jnp.concatenate([v_past.astype(v.dtype), v], axis=1)
        n_past = k_past.shape[1]
    o = stick_breaking(q, k_all, v_all, n_past, lp['sb_bias'])
    o = rms_norm(o, lp['sb_out_norm'].reshape(SB_HEADS, HEAD_DIM))
    return o.reshape(b, L, GROUP_WIDTH).astype(p.dtype), k, v


def causal_conv(xbc, conv_prev, w, bias):
    L = xbc.shape[1]
    full = jnp.concatenate([conv_prev.astype(xbc.dtype), xbc], axis=1)
    out = bias + sum(full[:, i:i + L] * w[i] for i in range(SSM_CONV))
    return out, full[:, L:]


def ssd_scan(x, dt, a_coef, b_in, c_in, h0):
    bsz, L, H, P = x.shape
    rep = H // SSM_GROUPS
    q = SSM_CHUNK if L % SSM_CHUNK == 0 else L
    nc = L // q
    chunk = lambda t: t.reshape(bsz, nc, q, *t.shape[2:])
    x_c, dt_c = chunk(x), chunk(dt)
    b_c = chunk(jnp.repeat(b_in, rep, axis=2))
    c_c = chunk(jnp.repeat(c_in, rep, axis=2))
    a_cum = jnp.cumsum(chunk(dt * a_coef), axis=2)
    a_t = jnp.moveaxis(a_cum, 2, 3)
    causal = jnp.tril(jnp.ones((q, q), dtype=bool))
    decay_ts = jnp.exp(jnp.where(causal, a_t[..., :, None] - a_t[..., None, :], -jnp.inf))
    scores = jnp.einsum('bcthn,bcshn->bchts', c_c, b_c) * decay_ts
    y_intra = jnp.einsum('bchts,bcsh,bcshp->bcthp', scores, dt_c, x_c)
    to_end = jnp.exp(a_cum[:, :, -1:, :] - a_cum) * dt_c
    chunk_states = jnp.einsum('bcsh,bcshn,bcshp->bchpn', to_end, b_c, x_c)
    chunk_decay = jnp.exp(a_cum[:, :, -1, :])

    def step(h, inp):
        st, dec = inp
        return h * dec[:, :, None, None] + st, h

    h_final, h_prev = lax.scan(step, h0, (jnp.moveaxis(chunk_states, 1, 0), jnp.moveaxis(chunk_decay, 1, 0)))
    h_prev = jnp.moveaxis(h_prev, 0, 1)
    y_inter = jnp.einsum('bcthn,bcth,bchpn->bcthp', c_c, jnp.exp(a_cum), h_prev)
    return (y_intra + y_inter).reshape(bsz, L, H, P), h_final


def mixer_ssd(p, lp, ssm0, conv0):
    b, L, _ = p.shape
    gw = GROUP_WIDTH
    z = p[..., :gw]
    xbc = p[..., gw:gw + SSM_CONV_DIM]
    dt_raw = p[..., gw + SSM_CONV_DIM:]
    if ssm0 is None:
        ssm0 = jnp.zeros((b, SSM_HEADS, HEAD_DIM, SSM_STATE), jnp.float32)
        conv0 = jnp.zeros((b, SSM_CONV - 1, SSM_CONV_DIM), p.dtype)
    xbc_c, conv_new = causal_conv(xbc, conv0, lp['conv_w'], lp['conv_b'])
    xbc_c = jax.nn.silu(xbc_c.astype(jnp.float32))
    gn = SSM_GROUPS * SSM_STATE
    xs = xbc_c[..., :gw].reshape(b, L, SSM_HEADS, HEAD_DIM)
    b_in = xbc_c[..., gw:gw + gn].reshape(b, L, SSM_GROUPS, SSM_STATE)
    c_in = xbc_c[..., gw + gn:].reshape(b, L, SSM_GROUPS, SSM_STATE)
    dt = jax.nn.softplus(dt_raw.astype(jnp.float32) + lp['dt_bias'].astype(jnp.float32))
    a_coef = -jnp.exp(lp['a_log'].astype(jnp.float32))
    y, h_final = ssd_scan(xs, dt, a_coef, b_in, c_in, ssm0.astype(jnp.float32))
    y = y + lp['d_skip'].astype(jnp.float32)[:, None] * xs
    y = y.reshape(b, L, gw) * jax.nn.silu(z.astype(jnp.float32))
    y = rms_norm(y, lp['ssm_norm'])
    return y.astype(p.dtype), h_final.astype(p.dtype), conv_new


def wkv7_recurrence(r, decay, k, v, kk, a, s0):
    def step(s, inp):
        r_t, w_t, k_t, v_t, kk_t, a_t = inp
        s_kk = jnp.einsum('bhvk,bhk->bhv', s, kk_t)
        s = (s * w_t[:, :, None, :]
             - jnp.einsum('bhv,bhk->bhvk', s_kk, kk_t * a_t)
             + jnp.einsum('bhv,bhk->bhvk', v_t, k_t))
        return s, jnp.einsum('bhvk,bhk->bhv', s, r_t)

    seq = tuple(jnp.moveaxis(t, 1, 0) for t in (r, decay, k, v, kk, a))
    s_final, ys = lax.scan(step, s0, seq)
    return jnp.moveaxis(ys, 0, 1), s_final


def mixer_rwkv7(p, lp, wkv0, shift0):
    b, L, _ = p.shape
    gw = GROUP_WIDTH
    if wkv0 is None:
        wkv0 = jnp.zeros((b, RWKV_HEADS, HEAD_DIM, HEAD_DIM), jnp.float32)
        shift0 = jnp.zeros((b, RWKV_PROJ), p.dtype)
    prev = jnp.concatenate([shift0[:, None, :].astype(p.dtype), p[:, :-1]], axis=1)
    xs = (p + (prev - p) * lp['shift_mu']).astype(jnp.float32)
    r, k, v = xs[..., :gw], xs[..., gw:2 * gw], xs[..., 2 * gw:3 * gw]
    o1 = 3 * gw
    o2 = o1 + DECAY_LORA
    o3 = o2 + ICLR_LORA
    w_lo, a_lo, g_lo = xs[..., o1:o2], xs[..., o2:o3], xs[..., o3:]
    w_log = -jax.nn.softplus(-(lp['decay_w0'] + jnp.tanh(w_lo) @ lp['decay_up'])) - 0.5
    decay = jnp.exp(-jnp.exp(w_log))
    a = jax.nn.sigmoid(lp['iclr_a0'] + a_lo @ lp['iclr_up'])
    g = jax.nn.sigmoid(g_lo) @ lp['gate_up']
    heads = lambda t: t.reshape(b, L, RWKV_HEADS, HEAD_DIM)
    kk = heads(k * lp['k_k'])
    kk = kk / jnp.maximum(jnp.linalg.norm(kk, axis=-1, keepdims=True), 1e-12)
    k = heads(k * (1.0 + (a - 1.0) * lp['k_a']))
    r, v, decay, a = heads(r), heads(v), heads(decay), heads(a)
    y, s_final = wkv7_recurrence(r, decay, k, v, kk, a, wkv0.astype(jnp.float32))
    mean = jnp.mean(y, axis=-1, keepdims=True)
    var = jnp.mean(jnp.square(y - mean), axis=-1, keepdims=True)
    y = ((y - mean) * lax.rsqrt(var + RWKV_GN_EPS)).reshape(b, L, gw) * lp['gn_w'] + lp['gn_b']
    bonus = jnp.sum(r * k * lp['r_k'], axis=-1, keepdims=True) * v
    y = (y + bonus.reshape(b, L, gw)) * g
    return y.astype(p.dtype), s_final.astype(p.dtype), p[:, -1]


def mixer_sgu(p, lp):
    b, L, _ = p.shape
    gw = GROUP_WIDTH
    u = jax.nn.gelu(p[..., :gw].astype(jnp.float32))
    v = jax.nn.gelu(p[..., gw:].astype(jnp.float32)).reshape(b, L, SGU_GROUPS, HEAD_DIM)
    v = rms_norm(v, lp['sgu_v_norm'].reshape(SGU_GROUPS, HEAD_DIM))
    q = SGU_CHUNK if L % SGU_CHUNK == 0 else L
    nc = L // q
    w = jnp.tril(lp['sgu_w'][:, :q, :q].astype(jnp.float32))
    bias = lp['sgu_b'][:, :q].astype(jnp.float32).T
    mixed = jnp.einsum('gts,bcsgd->bctgd', w, v.reshape(b, nc, q, SGU_GROUPS, HEAD_DIM)) + bias[None, None, :, :, None]
    out = u * mixed.reshape(b, L, gw)
    return out.astype(p.dtype), v.astype(p.dtype)


def swiglu(x2d, wg, wu, wd):
    return (jax.nn.silu(x2d @ wg) * (x2d @ wu)) @ wd


def moe_swiglu(x2d, router_w, wg, wu, wd):
    n_tok, d = x2d.shape
    logits = jnp.matmul(x2d, router_w).astype(jnp.float32)
    top_logit, top_e = lax.top_k(logits, TOP_K)
    gate = jax.nn.softmax(top_logit, axis=-1)
    n_assign = n_tok * TOP_K
    flat_e = top_e.reshape(n_assign)
    flat_tok = jnp.repeat(jnp.arange(n_tok, dtype=jnp.int32), TOP_K)
    order = jnp.argsort(flat_e)
    e_sorted = flat_e[order]
    counts = jnp.bincount(flat_e, length=N_EXPERTS)
    padded = (counts + MOE_BLOCK - 1) // MOE_BLOCK * MOE_BLOCK
    start = jnp.cumsum(counts) - counts
    end_padded = jnp.cumsum(padded)
    dest = (end_padded - padded)[e_sorted] + jnp.arange(n_assign) - start[e_sorted]
    n_blocks = -(-(n_assign + N_EXPERTS * (MOE_BLOCK - 1)) // MOE_BLOCK)
    n_rows = n_blocks * MOE_BLOCK
    row_tok = jnp.full((n_rows,), n_tok, jnp.int32).at[dest].set(flat_tok[order])
    row_gate = jnp.zeros((n_rows,), jnp.float32).at[dest].set(gate.reshape(n_assign)[order])
    block_e = jnp.minimum(jnp.searchsorted(end_padded, jnp.arange(n_blocks) * MOE_BLOCK, side='right'), N_EXPERTS - 1)
    x_pad = jnp.concatenate([x2d, jnp.zeros((1, d), x2d.dtype)], axis=0)

    def expert_block(args):
        tok, e = args
        xb = x_pad[tok]
        return (jax.nn.silu(xb @ wg[e]) * (xb @ wu[e])) @ wd[e]

    y_rows = lax.map(expert_block, (row_tok.reshape(n_blocks, MOE_BLOCK), block_e))
    y = jnp.zeros((n_tok + 1, d), jnp.float32).at[row_tok].add(
        y_rows.reshape(n_rows, d).astype(jnp.float32) * row_gate[:, None])
    return y[:n_tok].astype(x2d.dtype)


def trunk_layer(x, l, lp, past):
    h = rms_norm(x, lp['norm_mix'])
    proj = h @ lp['w_in']
    s1, s2, s3 = P_A, P_A + P_B, P_A + P_B + P_C
    pa, pb, pc, pd = proj[..., :s1], proj[..., s1:s2], proj[..., s2:s3], proj[..., s3:]
    oa, k_new, v_new = mixer_stick_breaking(pa, lp, past['k'], past['v'])
    ob, ssm_new, conv_new = mixer_ssd(pb, lp, past['ssm'], past['conv'])
    oc, wkv_new, shift_new = mixer_rwkv7(pc, lp, past['wkv'], past['shift'])
    od, sgu_v = mixer_sgu(pd, lp)
    x = x + jnp.concatenate([oa, ob, oc, od], axis=-1) @ lp['w_out']
    h2 = rms_norm(x, lp['norm_ffn']).reshape(-1, D_MODEL)
    if l % 2 == 0:
        f = swiglu(h2, lp['ffn_wg'], lp['ffn_wu'], lp['ffn_wd'])
    else:
        f = moe_swiglu(h2, lp['router_w'], lp['moe_wg'], lp['moe_wu'], lp['moe_wd'])
    x = x + f.reshape(x.shape)
    return x, (k_new, v_new, ssm_new, conv_new, wkv_new, shift_new, sgu_v)


def run_trunk(x, layer_params, layer_past):
    per_layer = []
    for l in range(DEPTH):
        x, st = trunk_layer(x, l, layer_params[l], layer_past[l])
        per_layer.append(st)
    return x, [jnp.stack(s) for s in zip(*per_layer)]


def setup_inputs(seed: int = 0) -> dict:
    key = jax.random.key(seed)
    keys = iter(jax.random.split(key, 64))
    f32 = jnp.float32

    def normal(shape, scale=1.0):
        return jax.random.normal(next(keys), shape, f32) * scale

    def gain(shape):
        return 1.0 + normal(shape, 0.05)

    n_pages = PAST_LEN // PAGE_SIZE
    n_pool = (DEC_BATCH * n_pages * 5) // 4
    page_table = jax.random.permutation(next(keys), n_pool)[:DEC_BATCH * n_pages].reshape(DEC_BATCH, n_pages).astype(jnp.int32)
    dt_init = jnp.exp(jax.random.uniform(next(keys), (DEPTH, SSM_HEADS), f32, math.log(1e-3), math.log(1e-1)))
    dt_bias = dt_init + jnp.log(-jnp.expm1(-dt_init))
    a_log = jnp.log(jax.random.uniform(next(keys), (DEPTH, SSM_HEADS), f32, 1.0, 16.0))
    shift_mu = jax.random.uniform(next(keys), (DEPTH, RWKV_PROJ), f32)
    return {
        'x_prompt': normal((BATCH, SEQ, D_MODEL)),
        'x_sample': normal((DEC_BATCH, DEC_SEQ, D_MODEL)),
        'cache_k': normal((DEPTH, n_pool, PAGE_SIZE, SB_HEADS, HEAD_DIM)),
        'cache_v': normal((DEPTH, n_pool, PAGE_SIZE, SB_HEADS, HEAD_DIM)),
        'page_table': page_table,
        'state_ssm': normal((DEPTH, DEC_BATCH, SSM_HEADS, HEAD_DIM, SSM_STATE), 0.1),
        'state_conv': normal((DEPTH, DEC_BATCH, SSM_CONV - 1, SSM_CONV_DIM)),
        'state_wkv': normal((DEPTH, DEC_BATCH, RWKV_HEADS, HEAD_DIM, HEAD_DIM), 0.1),
        'state_shift': normal((DEPTH, DEC_BATCH, RWKV_PROJ)),
        'norm_mix': gain((DEPTH, D_MODEL)),
        'norm_ffn': gain((DEPTH, D_MODEL)),
        'w_in': normal((DEPTH, D_MODEL, P_TOTAL), D_MODEL ** -0.5),
        'w_out': normal((DEPTH, D_MIX, D_MODEL), D_MIX ** -0.5),
        'q_norm': gain((DEPTH, HEAD_DIM)),
        'k_norm': gain((DEPTH, HEAD_DIM)),
        'sb_bias': SB_BIAS_INIT + normal((DEPTH, SB_HEADS), 0.1),
        'sb_out_norm': gain((DEPTH, GROUP_WIDTH)),
        'conv_w': normal((DEPTH, SSM_CONV, SSM_CONV_DIM), SSM_CONV ** -0.5),
        'conv_b': normal((DEPTH, SSM_CONV_DIM), 0.02),
        'dt_bias': dt_bias,
        'a_log': a_log,
        'd_skip': 1.0 + normal((DEPTH, SSM_HEADS), 0.1),
        'ssm_norm': gain((DEPTH, GROUP_WIDTH)),
        'shift_mu': shift_mu,
        'decay_w0': normal((DEPTH, GROUP_WIDTH), 0.5),
        'decay_up': normal((DEPTH, DECAY_LORA, GROUP_WIDTH), 0.5 * DECAY_LORA ** -0.5),
        'iclr_a0': normal((DEPTH, GROUP_WIDTH), 0.5),
        'iclr_up': normal((DEPTH, ICLR_LORA, GROUP_WIDTH), ICLR_LORA ** -0.5),
        'gate_up': normal((DEPTH, GATE_LORA, GROUP_WIDTH), GATE_LORA ** -0.5),
        'k_k': 0.85 + normal((DEPTH, GROUP_WIDTH), 0.05),
        'k_a': 1.0 + normal((DEPTH, GROUP_WIDTH), 0.05),
        'r_k': normal((DEPTH, RWKV_HEADS, HEAD_DIM), 0.1),
        'gn_w': gain((DEPTH, GROUP_WIDTH)),
        'gn_b': normal((DEPTH, GROUP_WIDTH), 0.02),
        'sgu_w': jnp.tril(normal((DEPTH, SGU_GROUPS, SGU_CHUNK, SGU_CHUNK), SGU_CHUNK ** -0.5)),
        'sgu_b': 1.0 + normal((DEPTH, SGU_GROUPS, SGU_CHUNK), 0.05),
        'sgu_v_norm': gain((DEPTH, GROUP_WIDTH)),
        'ffn_wg': normal((N_DENSE, D_MODEL, D_FF), D_MODEL ** -0.5),
        'ffn_wu': normal((N_DENSE, D_MODEL, D_FF), D_MODEL ** -0.5),
        'ffn_wd': normal((N_DENSE, D_FF, D_MODEL), D_FF ** -0.5),
        'router_w': normal((N_MOE, D_MODEL, N_EXPERTS), D_MODEL ** -0.5),
        'moe_wg': normal((N_MOE, N_EXPERTS, D_MODEL, D_FF_EXPERT), D_MODEL ** -0.5),
        'moe_wu': normal((N_MOE, N_EXPERTS, D_MODEL, D_FF_EXPERT), D_MODEL ** -0.5),
        'moe_wd': normal((N_MOE, N_EXPERTS, D_FF_EXPERT, D_MODEL), D_FF_EXPERT ** -0.5),
    }


def reference(x_prompt, x_sample, cache_k, cache_v, page_table, state_ssm, state_conv, state_wkv, state_shift,
              norm_mix, norm_ffn, w_in, w_out, q_norm, k_norm, sb_bias, sb_out_norm,
              conv_w, conv_b, dt_bias, a_log, d_skip, ssm_norm,
              shift_mu, decay_w0, decay_up, iclr_a0, iclr_up, gate_up, k_k, k_a, r_k, gn_w, gn_b,
              sgu_w, sgu_b, sgu_v_norm, ffn_wg, ffn_wu, ffn_wd, router_w, moe_wg, moe_wu, moe_wd):
    layer_params = []
    for l in range(DEPTH):
        lp = dict(norm_mix=norm_mix[l], norm_ffn=norm_ffn[l], w_in=w_in[l], w_out=w_out[l],
                  q_norm=q_norm[l], k_norm=k_norm[l], sb_bias=sb_bias[l], sb_out_norm=sb_out_norm[l],
                  conv_w=conv_w[l], conv_b=conv_b[l], dt_bias=dt_bias[l], a_log=a_log[l],
                  d_skip=d_skip[l], ssm_norm=ssm_norm[l],
                  shift_mu=shift_mu[l], decay_w0=decay_w0[l], decay_up=decay_up[l], iclr_a0=iclr_a0[l],
                  iclr_up=iclr_up[l], gate_up=gate_up[l], k_k=k_k[l], k_a=k_a[l], r_k=r_k[l],
                  gn_w=gn_w[l], gn_b=gn_b[l],
                  sgu_w=sgu_w[l], sgu_b=sgu_b[l], sgu_v_norm=sgu_v_norm[l])
        if l % 2 == 0:
            lp.update(ffn_wg=ffn_wg[l // 2], ffn_wu=ffn_wu[l // 2], ffn_wd=ffn_wd[l // 2])
        else:
            lp.update(router_w=router_w[l // 2], moe_wg=moe_wg[l // 2], moe_wu=moe_wu[l // 2], moe_wd=moe_wd[l // 2])
        layer_params.append(lp)

    prompt_past = [dict(k=None, v=None, ssm=None, conv=None, wkv=None, shift=None) for _ in range(DEPTH)]
    n_seq, n_pages = page_table.shape
    sample_past = []
    for l in range(DEPTH):
        sample_past.append(dict(
            k=cache_k[l][page_table].reshape(n_seq, n_pages * PAGE_SIZE, SB_HEADS, HEAD_DIM),
            v=cache_v[l][page_table].reshape(n_seq, n_pages * PAGE_SIZE, SB_HEADS, HEAD_DIM),
            ssm=state_ssm[l], conv=state_conv[l], wkv=state_wkv[l], shift=state_shift[l]))

    y_prompt, (k_p, v_p, ssm_p, conv_p, wkv_p, shift_p, _) = run_trunk(x_prompt, layer_params, prompt_past)
    y_sample, (k_s, v_s, ssm_s, conv_s, wkv_s, shift_s, sgu_v_s) = run_trunk(x_sample, layer_params, sample_past)
    return (y_prompt, y_sample, k_p, v_p, k_s, v_s, ssm_p, ssm_s, conv_p, conv_s, wkv_p, wkv_s, shift_p, shift_s, sgu_v_s)
```

```python
import functools

import jax
import jax.numpy as jnp
from jax import lax
from jax.experimental import pallas as pl
from jax.experimental.pallas import tpu as pltpu

F32, BF16, I32 = jnp.float32, jnp.bfloat16, jnp.int32

D_MODEL = 1024
HEAD_DIM = 64
HEADS = 4
GW = HEADS * HEAD_DIM
RMS_EPS = 1e-6
PAGE = 128
SSM_STATE = 128
SSM_GROUPS = 2
SSM_CONV = 4
SSM_CONV_DIM = GW + 2 * SSM_GROUPS * SSM_STATE
CHUNK = 128
RWKV_PROJ = 896
RWKV_LORA_LO = 3 * GW
RWKV_LORA_W = RWKV_PROJ - RWKV_LORA_LO
DECAY_LORA, ICLR_LORA, GATE_LORA = 32, 32, 64
RWKV_GN_EPS = HEAD_DIM * 1e-5
N_EXPERTS = 8
LANE = 128
SUBLANE = 8
VMEM_LIMIT = 52 * 1024 * 1024

_P_WIDTHS = (GW, GW, GW, GW, SSM_CONV_DIM, LANE, RWKV_PROJ, 2 * GW)
_P_TOTAL = sum(_P_WIDTHS)


def _cparams(sem):
    return pltpu.CompilerParams(dimension_semantics=sem, vmem_limit_bytes=VMEM_LIMIT)


def _const_spec(shape):
    nd = len(shape)
    return pl.BlockSpec(shape, lambda *_: (0,) * nd)


def _mm(a, b):
    return jnp.dot(a.astype(BF16), b.astype(BF16), preferred_element_type=F32)


def _mm_nt(a, b):
    return lax.dot_general(a.astype(BF16), b.astype(BF16), (((1,), (1,)), ((), ())),
                           preferred_element_type=F32)


def _parts(x, n):
    out, r = [], x
    for i in range(n):
        p = r.astype(BF16)
        out.append(p)
        if i + 1 < n:
            r = r - p.astype(F32)
    return out


def _mm_xl(a, b_exact, n):
    acc = None
    for p in _parts(a, n):
        t = jnp.dot(p, b_exact, preferred_element_type=F32)
        acc = t if acc is None else acc + t
    return acc


def _mm_xr(a_exact, b, n):
    acc = None
    for p in _parts(b, n):
        t = jnp.dot(a_exact, p, preferred_element_type=F32)
        acc = t if acc is None else acc + t
    return acc


def _seg_sum(x, bd):
    return _mm_xl(x, bd, 3)


def _sigmoid(x):
    return 1.0 / (1.0 + jnp.exp(-x))


def _silu(x):
    return x * _sigmoid(x)


def _softplus(x):
    return jnp.maximum(x, 0.0) + jnp.log1p(jnp.exp(-jnp.abs(x)))


def _gelu_tanh(x):
    return 0.5 * x * (1.0 + jnp.tanh(0.7978845608028654 * (x + 0.044715 * (x * x * x))))


def _rms(x, gain):
    return x * lax.rsqrt(jnp.mean(x * x, axis=-1, keepdims=True) + RMS_EPS) * gain


def _proj_kernel(x_ref, g_ref, w_ref, bd_ref, gq_ref, gk_ref,
                 q_out, k_out, v_out, z_out, xbc_out, dt_out, c_out, d_out):
    h = _rms(x_ref[...], g_ref[...]).astype(BF16)
    bd = bd_ref[...]
    outs = (q_out, k_out, v_out, z_out, xbc_out, dt_out, c_out, d_out)
    lo = 0
    for idx, (width, out) in enumerate(zip(_P_WIDTHS, outs)):
        y = jnp.dot(h, w_ref[:, lo:lo + width], preferred_element_type=F32)
        if idx < 2:
            gain = (gq_ref, gk_ref)[idx][...]
            y = y * lax.rsqrt(_seg_sum(y * y, bd) * (1.0 / HEAD_DIM) + RMS_EPS) * gain
        out[...] = y
        lo += width


def _proj(x2, g, w, bd, gq, gk, tm):
    n = x2.shape[0]
    return pl.pallas_call(
        _proj_kernel,
        grid=(n // tm,),
        in_specs=[pl.BlockSpec((tm, D_MODEL), lambda i: (i, 0)),
                  _const_spec((1, D_MODEL)), _const_spec((D_MODEL, _P_TOTAL)),
                  _const_spec((GW, GW)), _const_spec((1, GW)), _const_spec((1, GW))],
        out_specs=[pl.BlockSpec((tm, wd), lambda i: (i, 0)) for wd in _P_WIDTHS],
        out_shape=[jax.ShapeDtypeStruct((n, wd), F32) for wd in _P_WIDTHS],
        compiler_params=_cparams(("parallel",)),
        name="proj",
    )(x2, g, w, bd, gq, gk)


def _sb_weights(z, readable, u, carry):
    sp = jnp.maximum(z, 0.0) + jnp.log1p(jnp.exp(-jnp.abs(z)))
    lk = -sp if readable is None else jnp.where(readable, -sp, 0.0)
    tail = _mm_xl(lk, u, 2) + carry
    w = jnp.exp(z - sp + tail)
    if readable is not None:
        w = jnp.where(readable, w, 0.0)
    return w, jnp.sum(lk, axis=1, keepdims=True)


def _attn_prompt_kernel(bias_ref, q_ref, k_ref, v_ref, u_ref, gain_ref, o_ref, acc_ref, cs_ref, *, tq, tk):
    h = pl.program_id(1)
    i = pl.program_id(2)
    q = q_ref[0, 0].astype(BF16)
    bias = bias_ref[h]
    u = u_ref[...]
    acc_ref[...] = jnp.zeros_like(acc_ref)
    cs_ref[...] = jnp.zeros_like(cs_ref)
    n_chunks = (i + 1) * (tq // tk)
    q_pos = i * tq + lax.broadcasted_iota(I32, (tq, tk), 0)
    k_off = lax.broadcasted_iota(I32, (tq, tk), 1)

    def body(j, carry):
        ks = pl.multiple_of((n_chunks - 1 - j) * tk, tk)
        k = k_ref[0, 0, pl.ds(ks, tk), :].astype(BF16)
        v = v_ref[0, 0, pl.ds(ks, tk), :].astype(BF16)
        z = lax.dot_general(q, k, (((1,), (1,)), ((), ())), preferred_element_type=F32) + bias
        readable = (k_off + ks) < q_pos
        w, rs = _sb_weights(z, readable, u, cs_ref[...])
        acc_ref[...] += jnp.dot(w.astype(BF16), v, preferred_element_type=F32)
        cs_ref[...] += rs
        return carry

    lax.fori_loop(0, n_chunks, body, 0)
    o_ref[0, 0] = _rms(acc_ref[...], gain_ref[0])


def _attn_prompt(q, k, v, bias, gain, tq, tk):
    b, h, l, d = q.shape
    u = (jnp.arange(tk)[:, None] > jnp.arange(tk)[None, :]).astype(BF16)
    kern = functools.partial(_attn_prompt_kernel, tq=tq, tk=tk)
    return pl.pallas_call(
        kern,
        grid=(b, h, l // tq),
        in_specs=[pl.BlockSpec(memory_space=pltpu.SMEM),
                  pl.BlockSpec((1, 1, tq, d), lambda bi, hi, qi: (bi, hi, qi, 0)),
                  pl.BlockSpec((1, 1, l, d), lambda bi, hi, qi: (bi, hi, 0, 0)),
                  pl.BlockSpec((1, 1, l, d), lambda bi, hi, qi: (bi, hi, 0, 0)),
                  _const_spec((tk, tk)),
                  pl.BlockSpec((1, 1, d), lambda bi, hi, qi: (hi, 0, 0))],
        out_specs=pl.BlockSpec((1, 1, tq, d), lambda bi, hi, qi: (bi, hi, qi, 0)),
        out_shape=jax.ShapeDtypeStruct((b, h, l, d), F32),
        scratch_shapes=[pltpu.VMEM((tq, d), F32), pltpu.VMEM((tq, 1), F32)],
        compiler_params=_cparams(("parallel", "parallel", "arbitrary")),
        name="attn_prompt",
    )(bias, q, k, v, u, gain)


def _attn_sample_kernel(pt_ref, q_ref, bias_ref, kp_ref, vp_ref, kn_ref, vn_ref, u_ref, hm_ref, gain_ref,
                        o_ref, acc_ref, cs_ref, *, n_tok):
    p = pl.program_id(1)
    rows = HEADS * n_tok

    def process(k, v, readable):
        z = _mm_nt(q_ref[0], k) + bias_ref[...]
        w, rs = _sb_weights(z, readable, u_ref[...], cs_ref[...])
        acc_ref[...] += _mm(w, v)
        cs_ref[...] += rs

    @pl.when(p == 0)
    def _():
        acc_ref[...] = jnp.zeros_like(acc_ref)
        cs_ref[...] = jnp.zeros_like(cs_ref)
        tok = lax.rem(lax.broadcasted_iota(I32, (rows, PAGE), 0), n_tok)
        key = lax.broadcasted_iota(I32, (rows, PAGE), 1)
        process(kn_ref[0], vn_ref[0], key < tok)

    @pl.when(p > 0)
    def _():
        process(kp_ref[...], vp_ref[...], None)

    @pl.when(p == pl.num_programs(1) - 1)
    def _():
        o = acc_ref[...] * hm_ref[...]
        ss = jnp.sum(o * o, axis=1, keepdims=True) * (1.0 / HEAD_DIM)
        o_ref[0] = o * lax.rsqrt(ss + RMS_EPS) * gain_ref[...]


def _attn_sample(q16, bias16, cache_k, cache_v, layer, page_table, k_new, v_new, hm16, gain):
    s, rows, _ = q16.shape
    n_pages = page_table.shape[1]
    u = (jnp.arange(PAGE)[:, None] > jnp.arange(PAGE)[None, :]).astype(BF16)

    def page_map(si, p, pt):
        return (layer, pt[si, jnp.clip(n_pages - p, 0, n_pages - 1)], 0, 0)

    page_spec = pl.BlockSpec((None, None, PAGE, GW), page_map)
    seq_spec = lambda shape: pl.BlockSpec(shape, lambda si, p, pt: (si, 0, 0))
    cst = lambda shape: pl.BlockSpec(shape, lambda si, p, pt: (0,) * len(shape))
    kern = functools.partial(_attn_sample_kernel, n_tok=rows // HEADS)
    return pl.pallas_call(
        kern,
        grid_spec=pltpu.PrefetchScalarGridSpec(
            num_scalar_prefetch=1,
            grid=(s, n_pages + 1),
            in_specs=[seq_spec((1, rows, GW)), cst((rows, 1)), page_spec, page_spec,
                      seq_spec((1, PAGE, GW)), seq_spec((1, PAGE, GW)),
                      cst((PAGE, PAGE)), cst((rows, GW)), cst((1, GW))],
            out_specs=seq_spec((1, rows, GW)),
            scratch_shapes=[pltpu.VMEM((rows, GW), F32), pltpu.VMEM((rows, 1), F32)]),
        out_shape=jax.ShapeDtypeStruct((s, rows, GW), F32),
        compiler_params=_cparams(("parallel", "arbitrary")),
        name="attn_sample",
    )(page_table, q16, bias16, cache_k, cache_v, k_new, v_new, u, hm16, gain)


def _ssd_kernel(z_ref, x_ref, dt_ref, c0_ref, s0_ref, cw_ref, cb_ref, dtb_ref, a_ref, dsk_ref, nrm_ref, lt_ref,
                y_ref, st_out, xf_ref, st_ref, *, n_valid):
    t = CHUNK
    c = pl.program_id(1)

    @pl.when(c == 0)
    def _():
        xf_ref[0:SUBLANE, :] = c0_ref[0]
        st_ref[...] = s0_ref[0]

    xf_ref[SUBLANE:SUBLANE + t, :] = x_ref[0]
    conv = cb_ref[...]
    for i in range(SSM_CONV):
        conv = conv + cw_ref[i:i + 1, :] * xf_ref[pl.ds(SUBLANE - (SSM_CONV - 1) + i, t), :]
    xf_ref[0:SUBLANE, :] = xf_ref[t:t + SUBLANE, :]
    xc = _silu(conv)
    xs = xc[:, :GW]
    bm = xc[:, GW:GW + SSM_GROUPS * SSM_STATE]
    cm = xc[:, GW + SSM_GROUPS * SSM_STATE:]

    dt = _softplus(dt_ref[0] + dtb_ref[...])
    if n_valid < t:
        dt = jnp.where(lax.broadcasted_iota(I32, dt.shape, 0) < n_valid, dt, 0.0)
    acum = _mm_xr(lt_ref[...], dt * a_ref[...], 3)
    acum_t = acum.T
    dt_t = dt.T
    a_last = acum[t - 1:t, :]

    row = lax.broadcasted_iota(I32, (t, t), 0)
    col = lax.broadcasted_iota(I32, (t, t), 1)
    causal = row >= col
    lane = lax.broadcasted_iota(I32, (1, GW), 1)
    gmat = [_mm_nt(cm[:, g * SSM_STATE:(g + 1) * SSM_STATE], bm[:, g * SSM_STATE:(g + 1) * SSM_STATE])
            for g in range(SSM_GROUPS)]
    st_prev = st_ref[...]
    y = jnp.zeros((t, GW), F32)
    st_new = jnp.zeros((SSM_STATE, GW), F32)
    e_acum = jnp.zeros((t, GW), F32)
    a_end = jnp.zeros((1, GW), F32)
    for h in range(HEADS):
        g = h // (HEADS // SSM_GROUPS)
        mh = (lane // HEAD_DIM == h).astype(F32)
        col_h = acum[:, h:h + 1]
        decay = jnp.exp(jnp.where(causal, col_h - acum_t[h:h + 1, :], -1e30))
        scores = gmat[g] * decay * dt_t[h:h + 1, :]
        xm = xs * mh
        y = y + _mm(scores, xm)
        to_end = jnp.exp(a_last[:, h:h + 1] - col_h) * dt[:, h:h + 1]
        bw = bm[:, g * SSM_STATE:(g + 1) * SSM_STATE] * to_end
        st_new = st_new + _mm(bw.T, xm)
        e_acum = e_acum + mh * jnp.exp(col_h)
        a_end = a_end + mh * a_last[:, h:h + 1]
    y_in = jnp.where(lane < GW // SSM_GROUPS, _mm(cm[:, :SSM_STATE], st_prev), _mm(cm[:, SSM_STATE:], st_prev))
    st = st_prev * jnp.exp(a_end) + st_new
    st_ref[...] = st
    st_out[0] = st
    y = y + y_in * e_acum + dsk_ref[...] * xs
    y = y * _silu(z_ref[0])
    y_ref[0] = _rms(y, nrm_ref[...])


def _ssd(z, xbc, dt, conv0, st0, cw, cb, dtb, a_neg, dsk, nrm, n_valid):
    b, l, _ = z.shape
    lt = (jnp.arange(CHUNK)[:, None] >= jnp.arange(CHUNK)[None, :]).astype(BF16)
    blk = lambda wd: pl.BlockSpec((1, CHUNK, wd), lambda bi, ci: (bi, ci, 0))
    per_b = lambda shape: pl.BlockSpec(shape, lambda bi, ci: (bi, 0, 0))
    kern = functools.partial(_ssd_kernel, n_valid=n_valid)
    return pl.pallas_call(
        kern,
        grid=(b, l // CHUNK),
        in_specs=[blk(GW), blk(SSM_CONV_DIM), blk(LANE),
                  per_b((1, SUBLANE, SSM_CONV_DIM)), per_b((1, SSM_STATE, GW)),
                  _const_spec((SSM_CONV, SSM_CONV_DIM)), _const_spec((1, SSM_CONV_DIM)),
                  _const_spec((1, LANE)), _const_spec((1, LANE)), _const_spec((1, GW)), _const_spec((1, GW)),
                  _const_spec((CHUNK, CHUNK))],
        out_specs=[blk(GW), per_b((1, SSM_STATE, GW))],
        out_shape=[jax.ShapeDtypeStruct((b, l, GW), F32), jax.ShapeDtypeStruct((b, SSM_STATE, GW), F32)],
        scratch_shapes=[pltpu.VMEM((CHUNK + SUBLANE, SSM_CONV_DIM), F32), pltpu.VMEM((SSM_STATE, GW), F32)],
        compiler_params=_cparams(("parallel", "arbitrary")),
        name="ssd",
    )(z, xbc, dt, conv0, st0, cw, cb, dtb, a_neg, dsk, nrm, lt)


def _rwkv_kernel(p_ref, sh0_ref, s0_ref, mu_ref, w0_ref, a0_ref, dup_ref, iup_ref, gup_ref,
                 kk_ref, ka_ref, rk_ref, gnw_ref, gnb_ref, bd_ref, m64_ref,
                 y_ref, s_out,
                 pf_ref, st_ref, r_s, w_s, k_s, kk_s, ka_s, v_s, bon_s, g_s, q_s, y_s, *, bb, t, n_steps):
    c = pl.program_id(1)
    bd = bd_ref[...]
    m64 = m64_ref[...]

    @pl.when(c == 0)
    def _():
        for b in range(bb):
            pf_ref[b, 0:SUBLANE, :] = sh0_ref[b]
            st_ref[b] = s0_ref[b]

    if n_steps < t:
        q_s[...] = jnp.zeros_like(q_s)
        y_s[...] = jnp.zeros_like(y_s)

    for b in range(bb):
        pf_ref[b, SUBLANE:SUBLANE + t, :] = p_ref[b]
        cur = p_ref[b]
        prev = pf_ref[b, pl.ds(SUBLANE - 1, t), :]
        pf_ref[b, 0:SUBLANE, :] = pf_ref[b, t:t + SUBLANE, :]
        xs = cur + (prev - cur) * mu_ref[...]
        r = xs[:, 0:GW]
        k = xs[:, GW:2 * GW]
        v = xs[:, 2 * GW:3 * GW]
        lo = xs[:, RWKV_LORA_LO:RWKV_PROJ]
        w_log = -_softplus(-(w0_ref[...] + _mm(jnp.tanh(lo), dup_ref[...]))) - 0.5
        a = _sigmoid(a0_ref[...] + _mm(lo, iup_ref[...]))
        kkr = k * kk_ref[...]
        kk = kkr / jnp.maximum(jnp.sqrt(_seg_sum(kkr * kkr, bd)), 1e-12)
        kmod = k * (1.0 + (a - 1.0) * ka_ref[...])
        r_s[b] = r
        w_s[b] = jnp.exp(-jnp.exp(w_log))
        k_s[b] = kmod
        kk_s[b] = kk
        ka_s[b] = kk * a
        v_s[b] = v
        bon_s[b] = _seg_sum(r * kmod * rk_ref[...], bd) * v
        g_s[b] = _mm(_sigmoid(lo), gup_ref[...])

    def step(i, states):
        pieces = []
        for b in range(bb):
            prod = states[b] * kk_s[b, pl.ds(i, 1), :]
            hi = prod.astype(BF16)
            pieces += [hi, (prod - hi.astype(F32)).astype(BF16), (v_s[b, pl.ds(i, 1), :] * m64).astype(BF16)]
        res = jnp.dot(jnp.concatenate(pieces, axis=0), bd, preferred_element_type=F32)
        new_states = []
        for b in range(bb):
            o = 3 * HEAD_DIM * b
            s_kk = res[o:o + HEAD_DIM] + res[o + HEAD_DIM:o + 2 * HEAD_DIM]
            v_col = res[o + 2 * HEAD_DIM:o + 3 * HEAD_DIM]
            s_new = (states[b] * w_s[b, pl.ds(i, 1), :] - s_kk * ka_s[b, pl.ds(i, 1), :]
                     + v_col * k_s[b, pl.ds(i, 1), :])
            q_s[b, pl.ds(pl.multiple_of(i * HEAD_DIM, HEAD_DIM), HEAD_DIM), :] = s_new * r_s[b, pl.ds(i, 1), :]
            new_states.append(s_new)
        return tuple(new_states)

    states = lax.fori_loop(0, n_steps, step, tuple(st_ref[b] for b in range(bb)))

    for b in range(bb):
        st_ref[b] = states[b]
        s_out[b] = states[b]
        q_s[b] = _mm_xl(q_s[b], bd, 2)

    def read_out(i, carry):
        for b in range(bb):
            tile = q_s[b, pl.ds(pl.multiple_of(i * HEAD_DIM, HEAD_DIM), HEAD_DIM), :] * m64
            y_s[b, pl.ds(i, 1), :] = jnp.sum(tile, axis=0, keepdims=True)
        return carry

    lax.fori_loop(0, n_steps, read_out, 0)

    for b in range(bb):
        y = y_s[b]
        mean = _seg_sum(y, bd) * (1.0 / HEAD_DIM)
        yc = y - mean
        var = _seg_sum(yc * yc, bd) * (1.0 / HEAD_DIM)
        yn = yc * lax.rsqrt(var + RWKV_GN_EPS) * gnw_ref[...] + gnb_ref[...]
        y_ref[b] = (yn + bon_s[b]) * g_s[b]


def _rwkv(pc, sh0, s0, mu, w0, a0, dup, iup, gup, k_k, k_a, r_k, gnw, gnb, bd, m64, bb, t, n_steps):
    b, l, _ = pc.shape
    blk = lambda wd: pl.BlockSpec((bb, t, wd), lambda bi, ci: (bi, ci, 0))
    per_b = lambda shape: pl.BlockSpec(shape, lambda bi, ci: (bi, 0, 0))
    vec = lambda wd: _const_spec((1, wd))
    kern = functools.partial(_rwkv_kernel, bb=bb, t=t, n_steps=n_steps)
    row_scratch = [pltpu.VMEM((bb, t, GW), F32) for _ in range(8)]
    return pl.pallas_call(
        kern,
        grid=(b // bb, l // t),
        in_specs=[blk(RWKV_PROJ), per_b((bb, SUBLANE, RWKV_PROJ)), per_b((bb, HEAD_DIM, GW)),
                  vec(RWKV_PROJ), vec(GW), vec(GW),
                  _const_spec((RWKV_LORA_W, GW)), _const_spec((RWKV_LORA_W, GW)), _const_spec((RWKV_LORA_W, GW)),
                  vec(GW), vec(GW), vec(GW), vec(GW), vec(GW),
                  _const_spec((GW, GW)), _const_spec((HEAD_DIM, GW))],
        out_specs=[blk(GW), per_b((bb, HEAD_DIM, GW))],
        out_shape=[jax.ShapeDtypeStruct((b, l, GW), F32), jax.ShapeDtypeStruct((b, HEAD_DIM, GW), F32)],
        scratch_shapes=[pltpu.VMEM((bb, t + SUBLANE, RWKV_PROJ), F32), pltpu.VMEM((bb, HEAD_DIM, GW), F32)]
        + row_scratch
        + [pltpu.VMEM((bb, t * HEAD_DIM, GW), F32), pltpu.VMEM((bb, t, GW), F32)],
        compiler_params=_cparams(("parallel", "arbitrary")),
        name="rwkv",
    )(pc, sh0, s0, mu, w0, a0, dup, iup, gup, k_k, k_a, r_k, gnw, gnb, bd, m64)


def _sgu_kernel(p_ref, w_ref, bias_ref, gv_ref, bd_ref, o_ref, v_out):
    t = CHUNK
    p = p_ref[0]
    u = _gelu_tanh(p[:, :GW])
    v = _gelu_tanh(p[:, GW:])
    vn = v * lax.rsqrt(_seg_sum(v * v, bd_ref[...]) * (1.0 / HEAD_DIM) + RMS_EPS) * gv_ref[...]
    v_out[0] = vn
    causal = lax.broadcasted_iota(I32, (t, t), 0) >= lax.broadcasted_iota(I32, (t, t), 1)
    lane = lax.broadcasted_iota(I32, (1, GW), 1)
    mixed = bias_ref[...]
    for g in range(HEADS):
        wg = jnp.where(causal, w_ref[g], 0.0)
        mixed = mixed + _mm(wg, vn * (lane // HEAD_DIM == g).astype(F32))
    o_ref[0] = u * mixed


def _sgu(pd, w, bias, gv, bd):
    b, l, _ = pd.shape
    return pl.pallas_call(
        _sgu_kernel,
        grid=(b, l // CHUNK),
        in_specs=[pl.BlockSpec((1, CHUNK, 2 * GW), lambda bi, ci: (bi, ci, 0)),
                  _const_spec((HEADS, CHUNK, CHUNK)), _const_spec((CHUNK, GW)), _const_spec((1, GW)),
                  _const_spec((GW, GW))],
        out_specs=[pl.BlockSpec((1, CHUNK, GW), lambda bi, ci: (bi, ci, 0))] * 2,
        out_shape=[jax.ShapeDtypeStruct((b, l, GW), F32)] * 2,
        compiler_params=_cparams(("parallel", "parallel")),
        name="sgu",
    )(pd, w, bias, gv, bd)


def _outproj_kernel(x_ref, oa_ref, ob_ref, oc_ref, od_ref, w_ref, o_ref):
    acc = x_ref[...]
    for i, r in enumerate((oa_ref, ob_ref, oc_ref, od_ref)):
        acc = acc + jnp.dot(r[...].astype(BF16), w_ref[i * GW:(i + 1) * GW, :], preferred_element_type=F32)
    o_ref[...] = acc


def _outproj(x2, oa, ob, oc, od, w, tm):
    n = x2.shape[0]
    row = lambda wd: pl.BlockSpec((tm, wd), lambda i: (i, 0))
    return pl.pallas_call(
        _outproj_kernel,
        grid=(n // tm,),
        in_specs=[row(D_MODEL), row(GW), row(GW), row(GW), row(GW), _const_spec((4 * GW, D_MODEL))],
        out_specs=row(D_MODEL),
        out_shape=jax.ShapeDtypeStruct((n, D_MODEL), F32),
        compiler_params=_cparams(("parallel",)),
        name="outproj",
    )(x2, oa, ob, oc, od, w)


def _ffn_kernel(x_ref, g_ref, wg_ref, wu_ref, wd_ref, o_ref, h_s):
    f = pl.program_id(1)

    @pl.when(f == 0)
    def _():
        x = x_ref[...]
        h_s[...] = _rms(x, g_ref[...]).astype(BF16)
        o_ref[...] = x

    h = h_s[...]
    a = jnp.dot(h, wg_ref[...], preferred_element_type=F32)
    u = jnp.dot(h, wu_ref[...], preferred_element_type=F32)
    o_ref[...] += jnp.dot((_silu(a) * u).astype(BF16), wd_ref[...], preferred_element_type=F32)


def _ffn(x2, g, wg, wu, wd, tm, tf):
    n = x2.shape[0]
    d_ff = wg.shape[1]
    return pl.pallas_call(
        _ffn_kernel,
        grid=(n // tm, d_ff // tf),
        in_specs=[pl.BlockSpec((tm, D_MODEL), lambda i, f: (i, 0)), _const_spec((1, D_MODEL)),
                  pl.BlockSpec((D_MODEL, tf), lambda i, f: (0, f)),
                  pl.BlockSpec((D_MODEL, tf), lambda i, f: (0, f)),
                  pl.BlockSpec((tf, D_MODEL), lambda i, f: (f, 0))],
        out_specs=pl.BlockSpec((tm, D_MODEL), lambda i, f: (i, 0)),
        out_shape=jax.ShapeDtypeStruct((n, D_MODEL), F32),
        scratch_shapes=[pltpu.VMEM((tm, D_MODEL), BF16)],
        compiler_params=_cparams(("parallel", "arbitrary")),
        name="ffn",
    )(x2, g, wg, wu, wd)


def _router_kernel(x_ref, g_ref, rwt_ref, idx_ref, gate_ref):
    h = _rms(x_ref[...], g_ref[...])
    hh, hl = _parts(h, 2)
    wh, wl = _parts(rwt_ref[...], 2)
    nt = lambda a, b: lax.dot_general(a, b, (((1,), (1,)), ((), ())), preferred_element_type=F32)
    logits = nt(wh, hh) + (nt(wh, hl) + nt(wl, hh))
    e_id = lax.broadcasted_iota(I32, logits.shape, 0)
    m1 = jnp.max(logits, axis=0, keepdims=True)
    i1 = jnp.min(jnp.where(logits == m1, e_id, N_EXPERTS), axis=0, keepdims=True)
    rest = jnp.where(e_id == i1, -jnp.inf, logits)
    m2 = jnp.max(rest, axis=0, keepdims=True)
    i2 = jnp.min(jnp.where(rest == m2, e_id, N_EXPERTS), axis=0, keepdims=True)
    e = jnp.exp(m2 - m1)
    g1 = 1.0 / (1.0 + e)
    idx_ref[...] = jnp.where(e_id == 0, i1, jnp.where(e_id == 1, i2, 0))
    gate_ref[...] = jnp.where(e_id == 0, g1, jnp.where(e_id == 1, e * g1, 0.0))


def _router(x2, g, rwt, tm):
    n = x2.shape[0]
    return pl.pallas_call(
        _router_kernel,
        grid=(n // tm,),
        in_specs=[pl.BlockSpec((tm, D_MODEL), lambda i: (i, 0)), _const_spec((1, D_MODEL)),
                  _const_spec((N_EXPERTS, D_MODEL))],
        out_specs=[pl.BlockSpec((N_EXPERTS, tm), lambda i: (0, i))] * 2,
        out_shape=[jax.ShapeDtypeStruct((N_EXPERTS, n), I32), jax.ShapeDtypeStruct((N_EXPERTS, n), F32)],
        compiler_params=_cparams(("parallel",)),
        name="router",
    )(x2, g, rwt)


def _moe_kernel(tok_ref, off_ref, cnt_ref, gts_ref, x_ref, g_ref, wg_ref, wu_ref, wd_ref, o_ref,
                h_s, xg_s, og_s, *, gr):
    b = pl.program_id(0)
    e = pl.program_id(1)
    f = pl.program_id(2)
    n = cnt_ref[b * N_EXPERTS + e]
    o0 = off_ref[b * N_EXPERTS + e]

    @pl.when((e == 0) & (f == 0))
    def _():
        x = x_ref[...]
        h_s[...] = _rms(x, g_ref[...])
        o_ref[...] = x

    @pl.when((b == 0) & (e == 0) & (f == 0))
    def _():
        xg_s[...] = jnp.zeros_like(xg_s)

    @pl.when(f == 0)
    def _():
        def gather(i, carry):
            xg_s[pl.ds(i, 1), :] = h_s[pl.ds(tok_ref[o0 + i], 1), :]
            return carry
        lax.fori_loop(0, n, gather, 0)

    def group(gi, carry):
        r0 = pl.multiple_of(gi * gr, gr)
        xb = xg_s[pl.ds(r0, gr), :].astype(BF16)
        a = jnp.dot(xb, wg_ref[...], preferred_element_type=F32)
        u = jnp.dot(xb, wu_ref[...], preferred_element_type=F32)
        y = jnp.dot((_silu(a) * u).astype(BF16), wd_ref[...], preferred_element_type=F32)

        @pl.when(f == 0)
        def _():
            og_s[pl.ds(r0, gr), :] = y

        @pl.when(f > 0)
        def _():
            og_s[pl.ds(r0, gr), :] += y
        return carry

    lax.fori_loop(0, (n + gr - 1) // gr, group, 0)

    @pl.when(f == pl.num_programs(2) - 1)
    def _():
        def scatter(i, carry):
            tk = tok_ref[o0 + i]
            o_ref[pl.ds(tk, 1), :] += gts_ref[o0 + i] * og_s[pl.ds(i, 1), :]
            return carry
        lax.fori_loop(0, n, scatter, 0)


def _moe(x2, g, tok, off, cnt, gts, wg, wu, wd, tm, tf, gr):
    n = x2.shape[0]
    d_ff = wg.shape[2]
    kern = functools.partial(_moe_kernel, gr=gr)
    return pl.pallas_call(
        kern,
        grid_spec=pltpu.PrefetchScalarGridSpec(
            num_scalar_prefetch=3,
            grid=(n // tm, N_EXPERTS, d_ff // tf),
            in_specs=[pl.BlockSpec(memory_space=pltpu.SMEM),
                      pl.BlockSpec((tm, D_MODEL), lambda b, e, f, *_: (b, 0)),
                      pl.BlockSpec((1, D_MODEL), lambda b, e, f, *_: (0, 0)),
                      pl.BlockSpec((None, D_MODEL, tf), lambda b, e, f, *_: (e, 0, f)),
                      pl.BlockSpec((None, D_MODEL, tf), lambda b, e, f, *_: (e, 0, f)),
                      pl.BlockSpec((None, tf, D_MODEL), lambda b, e, f, *_: (e, f, 0))],
            out_specs=pl.BlockSpec((tm, D_MODEL), lambda b, e, f, *_: (b, 0)),
            scratch_shapes=[pltpu.VMEM((tm, D_MODEL), F32), pltpu.VMEM((tm, D_MODEL), F32),
                            pltpu.VMEM((tm, D_MODEL), F32)]),
        out_shape=jax.ShapeDtypeStruct((n, D_MODEL), F32),
        compiler_params=_cparams(("arbitrary", "arbitrary", "arbitrary")),
        name="moe",
    )(tok, off, cnt, gts, x2, g, wg, wu, wd)


def _moe_plan(idx, gate, tm):
    n = idx.shape[1]
    nb = n // tm
    e = idx[:2].T.reshape(nb, 2 * tm)
    gt = gate[:2].T.reshape(nb, 2 * tm)
    order = jnp.argsort(e, axis=1, stable=True)
    tok = (order // 2).astype(I32)
    gts = jnp.take_along_axis(gt, order, axis=1)
    cnt = jnp.sum(e[:, :, None] == jnp.arange(N_EXPERTS, dtype=I32)[None, None, :], axis=1).astype(I32)
    off = jnp.cumsum(cnt, axis=1) - cnt + (jnp.arange(nb, dtype=I32) * (2 * tm))[:, None]
    return tok.reshape(-1), off.reshape(-1).astype(I32), cnt.reshape(-1), gts.reshape(-1)


def _layer(x, l, p, consts, past, cache_k, cache_v, page_table):
    b, seq, _ = x.shape
    n = b * seq
    is_sample = past is not None
    tm = min(n, 256)
    x2 = x.reshape(n, D_MODEL)
    bd, m64, hm = consts["bd"], consts["m64"], consts["hm"]

    qn, kn, v, z, xbc, dtr, pc, pd = _proj(x2, p["norm_mix"], p["w_in"], bd, p["gq"], p["gk"], tm)
    k_new = kn.reshape(b, seq, HEADS, HEAD_DIM)
    v_new = v.reshape(b, seq, HEADS, HEAD_DIM)

    if not is_sample:
        to_heads = lambda a: a.reshape(b, seq, HEADS, HEAD_DIM).transpose(0, 2, 1, 3)
        tq = min(seq, 512)
        tk = min(seq, 256)
        oa = _attn_prompt(to_heads(qn), to_heads(kn), to_heads(v), p["sb_bias"],
                          p["sb_out_norm"].reshape(HEADS, 1, HEAD_DIM), tq, tk)
        oa = oa.transpose(0, 2, 1, 3).reshape(n, GW)
    else:
        rows = HEADS * seq
        q16 = (qn.reshape(b, 1, seq, GW) * hm[None, :, None, :]).reshape(b, rows, GW)
        pad_keys = lambda a: jnp.pad(a.reshape(b, seq, GW), ((0, 0), (0, PAGE - seq), (0, 0)))
        o16 = _attn_sample(q16, jnp.repeat(p["sb_bias"], seq).reshape(rows, 1), cache_k, cache_v, l, page_table,
                           pad_keys(kn), pad_keys(v), jnp.repeat(hm, seq, axis=0),
                           p["sb_out_norm"].reshape(1, GW))
        oa = o16.reshape(b, HEADS, seq, GW).sum(axis=1).reshape(n, GW)

    lc = -(-seq // CHUNK) * CHUNK
    pad_seq = lambda a, to: jnp.pad(a.reshape(b, seq, -1), ((0, 0), (0, to - seq), (0, 0)))

    xbc3 = xbc.reshape(b, seq, SSM_CONV_DIM)
    if is_sample:
        conv0 = past["conv"]
        st0 = past["ssm"].reshape(b, GW, SSM_STATE).transpose(0, 2, 1)
    else:
        conv0 = jnp.zeros((b, SSM_CONV - 1, SSM_CONV_DIM), F32)
        st0 = jnp.zeros((b, SSM_STATE, GW), F32)
    conv8 = jnp.pad(conv0, ((0, 0), (SUBLANE - (SSM_CONV - 1), 0), (0, 0)))
    ob, st = _ssd(pad_seq(z, lc), pad_seq(xbc, lc), pad_seq(dtr, lc), conv8, st0,
                  p["conv_w"], p["conv_b"], p["dt_bias"], p["a_neg"], p["d_skip"], p["ssm_norm"], min(seq, CHUNK))
    ob = ob[:, :seq].reshape(n, GW)
    ssm_new = st.transpose(0, 2, 1).reshape(b, HEADS, HEAD_DIM, SSM_STATE)
    conv_new = jnp.concatenate([conv0, xbc3], axis=1)[:, seq:]

    pc3 = pc.reshape(b, seq, RWKV_PROJ)
    if is_sample:
        shift0 = past["shift"]
        wkv0 = past["wkv"].transpose(0, 2, 1, 3).reshape(b, HEAD_DIM, GW)
    else:
        shift0 = jnp.zeros((b, RWKV_PROJ), F32)
        wkv0 = jnp.zeros((b, HEAD_DIM, GW), F32)
    sh8 = jnp.pad(shift0[:, None, :], ((0, 0), (SUBLANE - 1, 0), (0, 0)))
    t_r = min(-(-seq // SUBLANE) * SUBLANE, 64)
    lr = -(-seq // t_r) * t_r
    oc, wkv = _rwkv(pad_seq(pc, lr), sh8, wkv0, p["shift_mu"], p["decay_w0"], p["iclr_a0"],
                    p["decay_up"], p["iclr_up"], p["gate_up"], p["k_k"], p["k_a"], p["r_k"], p["gn_w"], p["gn_b"],
                    bd, m64, 2, t_r, min(seq, t_r))
    oc = oc[:, :seq].reshape(n, GW)
    wkv_new = wkv.reshape(b, HEAD_DIM, HEADS, HEAD_DIM).transpose(0, 2, 1, 3)
    shift_new = pc3[:, -1]

    od, sgu_v = _sgu(pad_seq(pd, lc), p["sgu_w"], p["sgu_bias"], p["sgu_v_norm"], bd)
    od = od[:, :seq].reshape(n, GW)
    sgu_v = sgu_v[:, :seq].reshape(b, seq, HEADS, HEAD_DIM)

    x_mid = _outproj(x2, oa, ob, oc, od, p["w_out"], tm)

    if "ffn_wg" in p:
        tmf = min(n, 512)
        x_out = _ffn(x_mid, p["norm_ffn"], p["ffn_wg"], p["ffn_wu"], p["ffn_wd"], tmf, p["ffn_wg"].shape[1] // 2)
    else:
        tmm = min(n, 1024)
        idx, gate = _router(x_mid, p["norm_ffn"], p["router_wt"], min(n, 512))
        tok, off, cnt, gts = _moe_plan(idx, gate, tmm)
        x_out = _moe(x_mid, p["norm_ffn"], tok, off, cnt, gts, p["moe_wg"], p["moe_wu"], p["moe_wd"],
                     tmm, 512, min(tmm, 256))
    return x_out.reshape(b, seq, D_MODEL), (k_new, v_new, ssm_new, conv_new, wkv_new, shift_new, sgu_v)


def _trunk(x, layers, consts, pasts, cache_k, cache_v, page_table):
    states = []
    for l, p in enumerate(layers):
        x, st = _layer(x, l, p, consts, None if pasts is None else pasts[l], cache_k, cache_v, page_table)
        states.append(st)
    return x, [jnp.stack(s) for s in zip(*states)]


def _pad_rows(w, lo, total):
    return jnp.pad(w, ((lo, total - lo - w.shape[0]), (0, 0)))


def kernel(x_prompt, x_sample, cache_k, cache_v, page_table, state_ssm, state_conv, state_wkv, state_shift, norm_mix, norm_ffn, w_in, w_out, q_norm, k_norm, sb_bias, sb_out_norm, conv_w, conv_b, dt_bias, a_log, d_skip, ssm_norm, shift_mu, decay_w0, decay_up, iclr_a0, iclr_up, gate_up, k_k, k_a, r_k, gn_w, gn_b, sgu_w, sgu_b, sgu_v_norm, ffn_wg, ffn_wu, ffn_wd, router_w, moe_wg, moe_wu, moe_wd):
    depth = w_in.shape[0]
    lane = jnp.arange(GW)
    consts = {
        "bd": (lane[:, None] // HEAD_DIM == lane[None, :] // HEAD_DIM).astype(BF16),
        "m64": (jnp.arange(HEAD_DIM)[:, None] == lane[None, :] % HEAD_DIM).astype(F32),
        "hm": (jnp.arange(HEADS)[:, None] == lane[None, :] // HEAD_DIM).astype(F32),
    }
    dt_col = 4 * GW + SSM_CONV_DIM
    row = lambda a: a.reshape(1, -1).astype(F32)
    layers = []
    for l in range(depth):
        w = w_in[l]
        w_packed = jnp.concatenate(
            [w[:, :dt_col + HEADS], jnp.zeros((D_MODEL, LANE - HEADS), F32), w[:, dt_col + HEADS:]], axis=1)
        p = dict(
            norm_mix=row(norm_mix[l]), norm_ffn=row(norm_ffn[l]),
            w_in=w_packed.astype(BF16), w_out=w_out[l].astype(BF16),
            gq=row(jnp.tile(q_norm[l], HEADS)) * (HEAD_DIM ** -0.5), gk=row(jnp.tile(k_norm[l], HEADS)),
            sb_bias=sb_bias[l].astype(F32), sb_out_norm=sb_out_norm[l],
            conv_w=conv_w[l], conv_b=row(conv_b[l]),
            dt_bias=jnp.pad(row(dt_bias[l]), ((0, 0), (0, LANE - HEADS))),
            a_neg=jnp.pad(row(-jnp.exp(a_log[l])), ((0, 0), (0, LANE - HEADS))),
            d_skip=row(jnp.repeat(d_skip[l], HEAD_DIM)), ssm_norm=row(ssm_norm[l]),
            shift_mu=row(shift_mu[l]), decay_w0=row(decay_w0[l]), iclr_a0=row(iclr_a0[l]),
            decay_up=_pad_rows(decay_up[l], 0, RWKV_LORA_W).astype(BF16),
            iclr_up=_pad_rows(iclr_up[l], DECAY_LORA, RWKV_LORA_W).astype(BF16),
            gate_up=_pad_rows(gate_up[l], DECAY_LORA + ICLR_LORA, RWKV_LORA_W).astype(BF16),
            k_k=row(k_k[l]), k_a=row(k_a[l]), r_k=row(r_k[l]), gn_w=row(gn_w[l]), gn_b=row(gn_b[l]),
            sgu_w=sgu_w[l][:, :CHUNK, :CHUNK],
            sgu_bias=jnp.repeat(sgu_b[l][:, :CHUNK].T, HEAD_DIM, axis=1),
            sgu_v_norm=row(sgu_v_norm[l]),
        )
        if l % 2 == 0:
            p.update(ffn_wg=ffn_wg[l // 2].astype(BF16), ffn_wu=ffn_wu[l // 2].astype(BF16),
                     ffn_wd=ffn_wd[l // 2].astype(BF16))
        else:
            p.update(router_wt=router_w[l // 2].T, moe_wg=moe_wg[l // 2].astype(BF16),
                     moe_wu=moe_wu[l // 2].astype(BF16), moe_wd=moe_wd[l // 2].astype(BF16))
        layers.append(p)

    ck = cache_k.reshape(cache_k.shape[0], cache_k.shape[1], PAGE, GW)
    cv = cache_v.reshape(cache_v.shape[0], cache_v.shape[1], PAGE, GW)
    pasts = [dict(ssm=state_ssm[l], conv=state_conv[l], wkv=state_wkv[l], shift=state_shift[l])
             for l in range(depth)]

    y_p, (k_p, v_p, ssm_p, conv_p, wkv_p, shift_p, _) = _trunk(x_prompt, layers, consts, None, ck, cv, page_table)
    y_s, (k_s, v_s, ssm_s, conv_s, wkv_s, shift_s, sgu_v_s) = _trunk(x_sample, layers, consts, pasts, ck, cv, page_table)
    return (y_p, y_s, k_p, v_p, k_s, v_s, ssm_p, ssm_s, conv_p, conv_s, wkv_p, wkv_s, shift_p, shift_s, sgu_v_s)
```

```python
import functools

import jax
import jax.numpy as jnp
from jax import lax
from jax.experimental import pallas as pl
from jax.experimental.pallas import tpu as pltpu

F32, BF16, I32 = jnp.float32, jnp.bfloat16, jnp.int32

D_MODEL = 1024
HEAD_DIM = 64
HEADS = 4
GW = HEADS * HEAD_DIM
RMS_EPS = 1e-6
PAGE = 128
SSM_STATE = 128
SSM_GROUPS = 2
SSM_CONV = 4
SSM_CONV_DIM = GW + 2 * SSM_GROUPS * SSM_STATE
CHUNK = 128
RWKV_PROJ = 896
RWKV_LORA_LO = 3 * GW
RWKV_LORA_W = RWKV_PROJ - RWKV_LORA_LO
DECAY_LORA, ICLR_LORA, GATE_LORA = 32, 32, 64
RWKV_GN_EPS = HEAD_DIM * 1e-5
N_EXPERTS = 8
LANE = 128
SUBLANE = 8
VMEM_LIMIT = 52 * 1024 * 1024
ATTN_STRIP = 128

_P_WIDTHS = (GW, GW, GW, GW, SSM_CONV_DIM, LANE, RWKV_PROJ, 2 * GW)
_P_TOTAL = sum(_P_WIDTHS)


def _cparams(sem):
    return pltpu.CompilerParams(dimension_semantics=sem, vmem_limit_bytes=VMEM_LIMIT)


def _const_spec(shape):
    nd = len(shape)
    return pl.BlockSpec(shape, lambda *_: (0,) * nd)


def _mm(a, b):
    return jnp.dot(a.astype(BF16), b.astype(BF16), preferred_element_type=F32)


def _mm_nt(a, b):
    return lax.dot_general(a.astype(BF16), b.astype(BF16), (((1,), (1,)), ((), ())),
                           preferred_element_type=F32)


def _parts(x, n):
    out, r = [], x
    for i in range(n):
        p = r.astype(BF16)
        out.append(p)
        if i + 1 < n:
            r = r - p.astype(F32)
    return out


def _mm_xl(a, b_exact, n):
    acc = None
    for p in _parts(a, n):
        t = jnp.dot(p, b_exact, preferred_element_type=F32)
        acc = t if acc is None else acc + t
    return acc


def _mm_xr(a_exact, b, n):
    acc = None
    for p in _parts(b, n):
        t = jnp.dot(a_exact, p, preferred_element_type=F32)
        acc = t if acc is None else acc + t
    return acc


def _seg_sum(x, bd):
    return _mm_xl(x, bd, 3)


def _sigmoid(x):
    return 1.0 / (1.0 + jnp.exp(-x))


def _silu(x):
    return x * _sigmoid(x)


def _softplus(x):
    return jnp.maximum(x, 0.0) + jnp.log1p(jnp.exp(-jnp.abs(x)))


def _gelu_tanh(x):
    return 0.5 * x * (1.0 + jnp.tanh(0.7978845608028654 * (x + 0.044715 * (x * x * x))))


def _rms(x, gain):
    return x * lax.rsqrt(jnp.mean(x * x, axis=-1, keepdims=True) + RMS_EPS) * gain


def _proj_kernel(x_ref, g_ref, w_ref, wdt_ref, bd_ref, gq_ref, gk_ref,
                 q_out, k_out, v_out, z_out, xbc_out, dt_out, c_out, d_out):
    h32 = _rms(x_ref[...], g_ref[...])
    h = h32.astype(BF16)
    bd = bd_ref[...]
    outs = (q_out, k_out, v_out, z_out, xbc_out, dt_out, c_out, d_out)
    lo = 0
    for idx, (width, out) in enumerate(zip(_P_WIDTHS, outs)):
        if out is dt_out:
            h_lo = (h32 - h.astype(F32)).astype(BF16)
            w_hi, w_lo = _parts(wdt_ref[...], 2)
            mm = lambda a, b: jnp.dot(a, b, preferred_element_type=F32)
            y = mm(h, w_hi) + (mm(h, w_lo) + mm(h_lo, w_hi))
        else:
            y = jnp.dot(h, w_ref[:, lo:lo + width], preferred_element_type=F32)
        if idx < 2:
            gain = (gq_ref, gk_ref)[idx][...]
            y = y * lax.rsqrt(_seg_sum(y * y, bd) * (1.0 / HEAD_DIM) + RMS_EPS) * gain
        out[...] = y
        lo += width


def _proj(x2, g, w, w_dt, bd, gq, gk, tm):
    n = x2.shape[0]
    return pl.pallas_call(
        _proj_kernel,
        grid=(n // tm,),
        in_specs=[pl.BlockSpec((tm, D_MODEL), lambda i: (i, 0)),
                  _const_spec((1, D_MODEL)), _const_spec((D_MODEL, _P_TOTAL)), _const_spec((D_MODEL, LANE)),
                  _const_spec((GW, GW)), _const_spec((1, GW)), _const_spec((1, GW))],
        out_specs=[pl.BlockSpec((tm, wd), lambda i: (i, 0)) for wd in _P_WIDTHS],
        out_shape=[jax.ShapeDtypeStruct((n, wd), F32) for wd in _P_WIDTHS],
        compiler_params=_cparams(("parallel",)),
        name="proj",
    )(x2, g, w, w_dt, bd, gq, gk)


def _sb_weights(zs, readables, u, carry):
    sps, lks = [], []
    for z, readable in zip(zs, readables):
        sp = jnp.maximum(z, 0.0) + jnp.log(1.0 + jnp.exp(-jnp.abs(z)))
        sps.append(sp)
        lks.append(-sp if readable is None else jnp.where(readable, -sp, 0.0))
    tails = [_mm_xl(lk, u, 2) for lk in lks]
    ws = []
    for z, readable, sp, lk, tail in zip(zs, readables, sps, lks, tails):
        w = jnp.exp(z - sp + (tail + carry))
        ws.append(w if readable is None else jnp.where(readable, w, 0.0))
        carry = carry + jnp.sum(lk, axis=1, keepdims=True)
    return ws, carry


def _attn_prompt_kernel(bias_ref, q_ref, k_ref, v_ref, u_ref, gain_ref, o_ref, acc_ref, cs_ref, *, tq, tk):
    h = pl.program_id(1)
    i = pl.program_id(2)
    bias = bias_ref[h]
    u = u_ref[...]
    acc_ref[...] = jnp.zeros_like(acc_ref)
    cs_ref[...] = jnp.zeros_like(cs_ref)
    q = q_ref[0, 0].astype(BF16)
    n_diag = tq // tk
    key_minus_query = lax.broadcasted_iota(I32, (tq, tk), 1) - lax.broadcasted_iota(I32, (tq, tk), 0)

    def chunk_group(c_hi, n, masked):
        zs, vs, readables = [], [], []
        for d in range(n):
            ks = pl.multiple_of((c_hi - d) * tk, tk)
            k = k_ref[0, 0, pl.ds(ks, tk), :].astype(BF16)
            vs.append(v_ref[0, 0, pl.ds(ks, tk), :].astype(BF16))
            zs.append(lax.dot_general(q, k, (((1,), (1,)), ((), ())), preferred_element_type=F32) + bias)
            readables.append((key_minus_query < i * tq - ks) if masked else None)
        ws, carry = _sb_weights(zs, readables, u, cs_ref[...])
        acc = acc_ref[...]
        for w, v in zip(ws, vs):
            acc = acc + jnp.dot(w.astype(BF16), v, preferred_element_type=F32)
        acc_ref[...] = acc
        cs_ref[...] = carry

    chunk_group((i + 1) * n_diag - 1, n_diag, True)

    def past_body(j, carry):
        chunk_group((i - j) * n_diag - 1, n_diag, False)
        return carry

    lax.fori_loop(0, i, past_body, 0)
    o_ref[0, 0] = _rms(acc_ref[...], gain_ref[0])


def _attn_prompt(q, k, v, bias, gain, tq, tk):
    b, h, l, d = q.shape
    u = (jnp.arange(tk)[:, None] > jnp.arange(tk)[None, :]).astype(BF16)
    kern = functools.partial(_attn_prompt_kernel, tq=tq, tk=tk)
    return pl.pallas_call(
        kern,
        grid=(b, h, l // tq),
        in_specs=[pl.BlockSpec(memory_space=pltpu.SMEM),
                  pl.BlockSpec((1, 1, tq, d), lambda bi, hi, qi: (bi, hi, qi, 0)),
                  pl.BlockSpec((1, 1, l, d), lambda bi, hi, qi: (bi, hi, 0, 0)),
                  pl.BlockSpec((1, 1, l, d), lambda bi, hi, qi: (bi, hi, 0, 0)),
                  _const_spec((tk, tk)),
                  pl.BlockSpec((1, 1, d), lambda bi, hi, qi: (hi, 0, 0))],
        out_specs=pl.BlockSpec((1, 1, tq, d), lambda bi, hi, qi: (bi, hi, qi, 0)),
        out_shape=jax.ShapeDtypeStruct((b, h, l, d), F32),
        scratch_shapes=[pltpu.VMEM((tq, d), F32), pltpu.VMEM((tq, 1), F32)],
        compiler_params=_cparams(("parallel", "parallel", "arbitrary")),
        name="attn_prompt",
    )(bias, q, k, v, u, gain)


def _attn_sample_kernel(pt_ref, q_ref, bias_ref, *refs, n_tok, n_group):
    kp_refs, vp_refs = refs[:n_group], refs[n_group:2 * n_group]
    kn_ref, vn_ref, u_ref, hm_ref, gain_ref, o_ref, acc_ref, cs_ref = refs[2 * n_group:]
    p = pl.program_id(1)
    rows = HEADS * n_tok
    q = q_ref[0].astype(BF16)

    def process(ks, vs, readables, acc, carry):
        zs = [_mm_nt(q, k) + bias_ref[...] for k in ks]
        ws, carry = _sb_weights(zs, readables, u_ref[...], carry)
        for w, v in zip(ws, vs):
            acc = acc + _mm(w, v)
        return acc, carry

    @pl.when(p == 0)
    def _():
        tok = lax.rem(lax.broadcasted_iota(I32, (rows, PAGE), 0), n_tok)
        key = lax.broadcasted_iota(I32, (rows, PAGE), 1)
        acc, carry = process([kn_ref[0]], [vn_ref[0]], [key < tok],
                             jnp.zeros(acc_ref.shape, F32), jnp.zeros(cs_ref.shape, F32))
        acc_ref[...] = acc
        cs_ref[...] = carry

    order = list(reversed(range(n_group)))
    acc, carry = process([kp_refs[j][...] for j in order], [vp_refs[j][...] for j in order], [None] * n_group,
                         acc_ref[...], cs_ref[...])
    acc_ref[...] = acc
    cs_ref[...] = carry

    @pl.when(p == pl.num_programs(1) - 1)
    def _():
        o = acc_ref[...] * hm_ref[...]
        ss = jnp.sum(o * o, axis=1, keepdims=True) * (1.0 / HEAD_DIM)
        o_ref[0] = o * lax.rsqrt(ss + RMS_EPS) * gain_ref[...]


def _attn_sample(q16, bias16, cache_k, cache_v, layer, page_table, k_new, v_new, hm16, gain):
    s, rows, _ = q16.shape
    n_pages = page_table.shape[1]
    u = (jnp.arange(PAGE)[:, None] > jnp.arange(PAGE)[None, :]).astype(BF16)

    n_group = next(g for g in (8, 4, 2, 1) if n_pages % g == 0)

    def page_spec(j):
        return pl.BlockSpec((None, None, PAGE, GW),
                            lambda si, p, pt: (layer, pt[si, n_pages - (p + 1) * n_group + j], 0, 0))

    page_specs = [page_spec(j) for j in range(n_group)]
    seq_spec = lambda shape: pl.BlockSpec(shape, lambda si, p, pt: (si, 0, 0))
    cst = lambda shape: pl.BlockSpec(shape, lambda si, p, pt: (0,) * len(shape))
    kern = functools.partial(_attn_sample_kernel, n_tok=rows // HEADS, n_group=n_group)
    return pl.pallas_call(
        kern,
        grid_spec=pltpu.PrefetchScalarGridSpec(
            num_scalar_prefetch=1,
            grid=(s, n_pages // n_group),
            in_specs=[seq_spec((1, rows, GW)), cst((rows, 1))] + page_specs + page_specs
            + [seq_spec((1, PAGE, GW)), seq_spec((1, PAGE, GW)),
               cst((PAGE, PAGE)), cst((rows, GW)), cst((1, GW))],
            out_specs=seq_spec((1, rows, GW)),
            scratch_shapes=[pltpu.VMEM((rows, GW), F32), pltpu.VMEM((rows, 1), F32)]),
        out_shape=jax.ShapeDtypeStruct((s, rows, GW), F32),
        compiler_params=_cparams(("parallel", "arbitrary")),
        name="attn_sample",
    )(page_table, q16, bias16, *([cache_k] * n_group), *([cache_v] * n_group), k_new, v_new, u, hm16, gain)


def _ssd_kernel(z_ref, x_ref, dt_ref, c0_ref, s0_ref, cw_ref, cb_ref, dtb_ref, a_ref, dsk_ref, nrm_ref, lt_ref,
                y_ref, st_out, xf_ref, st_ref, *, n_valid):
    t = CHUNK
    c = pl.program_id(1)

    @pl.when(c == 0)
    def _():
        xf_ref[0:SUBLANE, :] = c0_ref[0]
        st_ref[...] = s0_ref[0]

    xf_ref[SUBLANE:SUBLANE + t, :] = x_ref[0]
    conv = cb_ref[...]
    for i in range(SSM_CONV):
        conv = conv + cw_ref[i:i + 1, :] * xf_ref[pl.ds(SUBLANE - (SSM_CONV - 1) + i, t), :]
    xf_ref[0:SUBLANE, :] = xf_ref[t:t + SUBLANE, :]
    xc = _silu(conv)
    xs = xc[:, :GW]
    bm = xc[:, GW:GW + SSM_GROUPS * SSM_STATE]
    cm = xc[:, GW + SSM_GROUPS * SSM_STATE:]

    dt = _softplus(dt_ref[0] + dtb_ref[...])
    if n_valid < t:
        dt = jnp.where(lax.broadcasted_iota(I32, dt.shape, 0) < n_valid, dt, 0.0)
    acum = _mm_xr(lt_ref[...], dt * a_ref[...], 3)
    acum_t = acum.T
    dt_t = dt.T
    a_last = acum[t - 1:t, :]

    row = lax.broadcasted_iota(I32, (t, t), 0)
    col = lax.broadcasted_iota(I32, (t, t), 1)
    causal = row >= col
    lane = lax.broadcasted_iota(I32, (1, GW), 1)
    gmat = [_mm_nt(cm[:, g * SSM_STATE:(g + 1) * SSM_STATE], bm[:, g * SSM_STATE:(g + 1) * SSM_STATE])
            for g in range(SSM_GROUPS)]
    st_prev = st_ref[...]
    y = jnp.zeros((t, GW), F32)
    st_new = jnp.zeros((SSM_STATE, GW), F32)
    e_acum = jnp.zeros((t, GW), F32)
    a_end = jnp.zeros((1, GW), F32)
    for h in range(HEADS):
        g = h // (HEADS // SSM_GROUPS)
        mh = (lane // HEAD_DIM == h).astype(F32)
        col_h = acum[:, h:h + 1]
        decay = jnp.exp(jnp.where(causal, col_h - acum_t[h:h + 1, :], -1e30))
        scores = gmat[g] * decay * dt_t[h:h + 1, :]
        xm = xs * mh
        y = y + _mm(scores, xm)
        to_end = jnp.exp(a_last[:, h:h + 1] - col_h) * dt[:, h:h + 1]
        bw = bm[:, g * SSM_STATE:(g + 1) * SSM_STATE] * to_end
        st_new = st_new + _mm(bw.T, xm)
        e_acum = e_acum + mh * jnp.exp(col_h)
        a_end = a_end + mh * a_last[:, h:h + 1]
    y_in = jnp.where(lane < GW // SSM_GROUPS, _mm(cm[:, :SSM_STATE], st_prev), _mm(cm[:, SSM_STATE:], st_prev))
    st = st_prev * jnp.exp(a_end) + st_new
    st_ref[...] = st
    st_out[0] = st
    y = y + y_in * e_acum + dsk_ref[...] * xs
    y = y * _silu(z_ref[0])
    y_ref[0] = _rms(y, nrm_ref[...])


def _ssd(z, xbc, dt, conv0, st0, cw, cb, dtb, a_neg, dsk, nrm, n_valid):
    b, l, _ = z.shape
    lt = (jnp.arange(CHUNK)[:, None] >= jnp.arange(CHUNK)[None, :]).astype(BF16)
    blk = lambda wd: pl.BlockSpec((1, CHUNK, wd), lambda bi, ci: (bi, ci, 0))
    per_b = lambda shape: pl.BlockSpec(shape, lambda bi, ci: (bi, 0, 0))
    kern = functools.partial(_ssd_kernel, n_valid=n_valid)
    return pl.pallas_call(
        kern,
        grid=(b, l // CHUNK),
        in_specs=[blk(GW), blk(SSM_CONV_DIM), blk(LANE),
                  per_b((1, SUBLANE, SSM_CONV_DIM)), per_b((1, SSM_STATE, GW)),
                  _const_spec((SSM_CONV, SSM_CONV_DIM)), _const_spec((1, SSM_CONV_DIM)),
                  _const_spec((1, LANE)), _const_spec((1, LANE)), _const_spec((1, GW)), _const_spec((1, GW)),
                  _const_spec((CHUNK, CHUNK))],
        out_specs=[blk(GW), per_b((1, SSM_STATE, GW))],
        out_shape=[jax.ShapeDtypeStruct((b, l, GW), F32), jax.ShapeDtypeStruct((b, SSM_STATE, GW), F32)],
        scratch_shapes=[pltpu.VMEM((CHUNK + SUBLANE, SSM_CONV_DIM), F32), pltpu.VMEM((SSM_STATE, GW), F32)],
        compiler_params=_cparams(("parallel", "arbitrary")),
        name="ssd",
    )(z, xbc, dt, conv0, st0, cw, cb, dtb, a_neg, dsk, nrm, lt)


def _rwkv_kernel(p_ref, sh0_ref, s0_ref, mu_ref, w0_ref, a0_ref, dup_ref, iup_ref, gup_ref,
                 kk_ref, ka_ref, rk_ref, gnw_ref, gnb_ref, bd_ref, lt_ref,
                 y_ref, s_out, pf_ref, st_ref, *, bb, t, n_valid):
    c = pl.program_id(1)
    bd = bd_ref[...]
    lane = lax.broadcasted_iota(I32, (1, GW), 1)
    head_mask = [(lane // HEAD_DIM == h).astype(F32) for h in range(HEADS)]
    stack = lambda x: jnp.concatenate([x * m for m in head_mask], axis=0)
    unstack = lambda x: sum(x[h * t:(h + 1) * t] for h in range(HEADS))
    step_r = lax.broadcasted_iota(I32, (HEADS * t, HEADS * t), 0) % t
    step_c = lax.broadcasted_iota(I32, (HEADS * t, HEADS * t), 1) % t
    nt = lambda x, y: lax.dot_general(x, y, (((1,), (1,)), ((), ())), preferred_element_type=F32)
    tn = lambda x, y: lax.dot_general(x, y, (((0,), (0,)), ((), ())), preferred_element_type=F32)
    mm = lambda x, y: jnp.dot(x, y, preferred_element_type=F32)

    @pl.when(c == 0)
    def _():
        for b in range(bb):
            pf_ref[b, 0:SUBLANE, :] = sh0_ref[b]
            st_ref[b] = stack(s0_ref[b])

    for b in range(bb):
        pf_ref[b, SUBLANE:SUBLANE + t, :] = p_ref[b]
        cur = p_ref[b]
        prev = pf_ref[b, pl.ds(SUBLANE - 1, t), :]
        pf_ref[b, 0:SUBLANE, :] = pf_ref[b, t:t + SUBLANE, :]
        xs = cur + (prev - cur) * mu_ref[...]
        r = xs[:, 0:GW]
        k = xs[:, GW:2 * GW]
        v = xs[:, 2 * GW:3 * GW]
        lo = xs[:, RWKV_LORA_LO:RWKV_PROJ]
        w_log = -_softplus(-(w0_ref[...] + _mm(jnp.tanh(lo), dup_ref[...]))) - 0.5
        a = _sigmoid(a0_ref[...] + _mm(lo, iup_ref[...]))
        kkr = k * kk_ref[...]
        kk = kkr / jnp.maximum(jnp.sqrt(_seg_sum(kkr * kkr, bd)), 1e-12)
        kmod = k * (1.0 + (a - 1.0) * ka_ref[...])
        ka = kk * a
        log_w = -jnp.exp(w_log)
        if n_valid < t:
            valid = lax.broadcasted_iota(I32, (t, GW), 0) < n_valid
            log_w = jnp.where(valid, log_w, 0.0)
            kk, ka, kmod, v = (jnp.where(valid, x, 0.0) for x in (kk, ka, kmod, v))
        bonus = _seg_sum(r * kmod * rk_ref[...], bd) * v
        gate = _mm(_sigmoid(lo), gup_ref[...])

        cum = _mm_xr(lt_ref[...], log_w, 3)
        inv_g = jnp.exp(-cum)
        a_s = stack(kk * jnp.exp(cum - log_w)).astype(BF16)
        b_s = stack(ka * inv_g).astype(BF16)
        k_s = stack(kmod * inv_g).astype(BF16)
        r_s = stack(r * jnp.exp(cum)).astype(BF16)
        v_s = stack(v).astype(BF16)
        s0 = st_ref[b]
        s0_b = s0.astype(BF16)

        strict = step_r > step_c
        incl = step_r >= step_c
        neg_l = jnp.where(strict, -nt(a_s, b_s), 0.0).astype(BF16)
        l_ak = jnp.where(strict, nt(a_s, k_s), 0.0).astype(BF16)
        l_rk = jnp.where(incl, nt(r_s, k_s), 0.0).astype(BF16)
        l_rb = jnp.where(incl, nt(r_s, b_s), 0.0).astype(BF16)
        u = nt(a_s, s0_b) + mm(l_ak, v_s)
        power = neg_l
        u = u + mm(power, u.astype(BF16))
        for _ in range(t.bit_length() - 2):
            power = mm(power, power).astype(BF16)
            u = u + mm(power, u.astype(BF16))
        u_b = u.astype(BF16)
        y = unstack(nt(r_s, s0_b) + mm(l_rk, v_s) - mm(l_rb, u_b))
        st_ref[b] = (s0 + tn(v_s, k_s) - tn(u_b, b_s)) * jnp.exp(cum[t - 1:t, :])

        mean = _seg_sum(y, bd) * (1.0 / HEAD_DIM)
        yc = y - mean
        var = _seg_sum(yc * yc, bd) * (1.0 / HEAD_DIM)
        yn = yc * lax.rsqrt(var + RWKV_GN_EPS) * gnw_ref[...] + gnb_ref[...]
        y_ref[b] = (yn + bonus) * gate
        s_out[b] = unstack(st_ref[b])


def _rwkv(pc, sh0, s0, mu, w0, a0, dup, iup, gup, k_k, k_a, r_k, gnw, gnb, bd, bb, n_valid):
    b, l, _ = pc.shape
    t = HEAD_DIM
    lt = (jnp.arange(t)[:, None] >= jnp.arange(t)[None, :]).astype(BF16)
    blk = lambda wd: pl.BlockSpec((bb, t, wd), lambda bi, ci: (bi, ci, 0))
    per_b = lambda shape: pl.BlockSpec(shape, lambda bi, ci: (bi, 0, 0))
    vec = lambda wd: _const_spec((1, wd))
    kern = functools.partial(_rwkv_kernel, bb=bb, t=t, n_valid=n_valid)
    return pl.pallas_call(
        kern,
        grid=(b // bb, l // t),
        in_specs=[blk(RWKV_PROJ), per_b((bb, SUBLANE, RWKV_PROJ)), per_b((bb, HEAD_DIM, GW)),
                  vec(RWKV_PROJ), vec(GW), vec(GW),
                  _const_spec((RWKV_LORA_W, GW)), _const_spec((RWKV_LORA_W, GW)), _const_spec((RWKV_LORA_W, GW)),
                  vec(GW), vec(GW), vec(GW), vec(GW), vec(GW),
                  _const_spec((GW, GW)), _const_spec((t, t))],
        out_specs=[blk(GW), per_b((bb, HEAD_DIM, GW))],
        out_shape=[jax.ShapeDtypeStruct((b, l, GW), F32), jax.ShapeDtypeStruct((b, HEAD_DIM, GW), F32)],
        scratch_shapes=[pltpu.VMEM((bb, t + SUBLANE, RWKV_PROJ), F32), pltpu.VMEM((bb, HEADS * HEAD_DIM, GW), F32)],
        compiler_params=_cparams(("parallel", "arbitrary")),
        name="rwkv",
    )(pc, sh0, s0, mu, w0, a0, dup, iup, gup, k_k, k_a, r_k, gnw, gnb, bd, lt)


def _sgu_kernel(p_ref, w_ref, bias_ref, gv_ref, bd_ref, o_ref, v_out):
    t = CHUNK
    p = p_ref[0]
    u = _gelu_tanh(p[:, :GW])
    v = _gelu_tanh(p[:, GW:])
    vn = v * lax.rsqrt(_seg_sum(v * v, bd_ref[...]) * (1.0 / HEAD_DIM) + RMS_EPS) * gv_ref[...]
    v_out[0] = vn
    causal = lax.broadcasted_iota(I32, (t, t), 0) >= lax.broadcasted_iota(I32, (t, t), 1)
    lane = lax.broadcasted_iota(I32, (1, GW), 1)
    mixed = bias_ref[...]
    for g in range(HEADS):
        wg = jnp.where(causal, w_ref[g], 0.0)
        mixed = mixed + _mm(wg, vn * (lane // HEAD_DIM == g).astype(F32))
    o_ref[0] = u * mixed


def _sgu(pd, w, bias, gv, bd):
    b, l, _ = pd.shape
    return pl.pallas_call(
        _sgu_kernel,
        grid=(b, l // CHUNK),
        in_specs=[pl.BlockSpec((1, CHUNK, 2 * GW), lambda bi, ci: (bi, ci, 0)),
                  _const_spec((HEADS, CHUNK, CHUNK)), _const_spec((CHUNK, GW)), _const_spec((1, GW)),
                  _const_spec((GW, GW))],
        out_specs=[pl.BlockSpec((1, CHUNK, GW), lambda bi, ci: (bi, ci, 0))] * 2,
        out_shape=[jax.ShapeDtypeStruct((b, l, GW), F32)] * 2,
        compiler_params=_cparams(("parallel", "parallel")),
        name="sgu",
    )(pd, w, bias, gv, bd)


def _outproj_kernel(x_ref, oa_ref, ob_ref, oc_ref, od_ref, w_ref, o_ref):
    acc = x_ref[...]
    for i, r in enumerate((oa_ref, ob_ref, oc_ref, od_ref)):
        acc = acc + jnp.dot(r[...].astype(BF16), w_ref[i * GW:(i + 1) * GW, :], preferred_element_type=F32)
    o_ref[...] = acc


def _outproj(x2, oa, ob, oc, od, w, tm):
    n = x2.shape[0]
    row = lambda wd: pl.BlockSpec((tm, wd), lambda i: (i, 0))
    return pl.pallas_call(
        _outproj_kernel,
        grid=(n // tm,),
        in_specs=[row(D_MODEL), row(GW), row(GW), row(GW), row(GW), _const_spec((4 * GW, D_MODEL))],
        out_specs=row(D_MODEL),
        out_shape=jax.ShapeDtypeStruct((n, D_MODEL), F32),
        compiler_params=_cparams(("parallel",)),
        name="outproj",
    )(x2, oa, ob, oc, od, w)


def _ffn_kernel(x_ref, g_ref, wg_ref, wu_ref, wd_ref, o_ref, h_s):
    f = pl.program_id(1)

    @pl.when(f == 0)
    def _():
        x = x_ref[...]
        h_s[...] = _rms(x, g_ref[...]).astype(BF16)
        o_ref[...] = x

    h = h_s[...]
    a = jnp.dot(h, wg_ref[...], preferred_element_type=F32)
    u = jnp.dot(h, wu_ref[...], preferred_element_type=F32)
    o_ref[...] += jnp.dot((_silu(a) * u).astype(BF16), wd_ref[...], preferred_element_type=F32)


def _ffn(x2, g, wg, wu, wd, tm, tf):
    n = x2.shape[0]
    d_ff = wg.shape[1]
    return pl.pallas_call(
        _ffn_kernel,
        grid=(n // tm, d_ff // tf),
        in_specs=[pl.BlockSpec((tm, D_MODEL), lambda i, f: (i, 0)), _const_spec((1, D_MODEL)),
                  pl.BlockSpec((D_MODEL, tf), lambda i, f: (0, f)),
                  pl.BlockSpec((D_MODEL, tf), lambda i, f: (0, f)),
                  pl.BlockSpec((tf, D_MODEL), lambda i, f: (f, 0))],
        out_specs=pl.BlockSpec((tm, D_MODEL), lambda i, f: (i, 0)),
        out_shape=jax.ShapeDtypeStruct((n, D_MODEL), F32),
        scratch_shapes=[pltpu.VMEM((tm, D_MODEL), BF16)],
        compiler_params=_cparams(("parallel", "arbitrary")),
        name="ffn",
    )(x2, g, wg, wu, wd)


def _router_kernel(x_ref, g_ref, rwt_ref, idx_ref, gate_ref):
    h = _rms(x_ref[...], g_ref[...])
    hh, hl = _parts(h, 2)
    wh, wl = _parts(rwt_ref[...], 2)
    nt = lambda a, b: lax.dot_general(a, b, (((1,), (1,)), ((), ())), preferred_element_type=F32)
    logits = nt(wh, hh) + (nt(wh, hl) + nt(wl, hh))
    e_id = lax.broadcasted_iota(I32, logits.shape, 0)
    m1 = jnp.max(logits, axis=0, keepdims=True)
    i1 = jnp.min(jnp.where(logits == m1, e_id, N_EXPERTS), axis=0, keepdims=True)
    rest = jnp.where(e_id == i1, -jnp.inf, logits)
    m2 = jnp.max(rest, axis=0, keepdims=True)
    i2 = jnp.min(jnp.where(rest == m2, e_id, N_EXPERTS), axis=0, keepdims=True)
    e = jnp.exp(m2 - m1)
    g1 = 1.0 / (1.0 + e)
    idx_ref[...] = jnp.where(e_id == 0, i1, jnp.where(e_id == 1, i2, 0))
    gate_ref[...] = jnp.where(e_id == 0, g1, jnp.where(e_id == 1, e * g1, 0.0))


def _router(x2, g, rwt, tm):
    n = x2.shape[0]
    return pl.pallas_call(
        _router_kernel,
        grid=(n // tm,),
        in_specs=[pl.BlockSpec((tm, D_MODEL), lambda i: (i, 0)), _const_spec((1, D_MODEL)),
                  _const_spec((N_EXPERTS, D_MODEL))],
        out_specs=[pl.BlockSpec((N_EXPERTS, tm), lambda i: (0, i))] * 2,
        out_shape=[jax.ShapeDtypeStruct((N_EXPERTS, n), I32), jax.ShapeDtypeStruct((N_EXPERTS, n), F32)],
        compiler_params=_cparams(("parallel",)),
        name="router",
    )(x2, g, rwt)


def _moe_kernel(tok_ref, off_ref, cnt_ref, gts_ref, x_ref, g_ref, wg_ref, wu_ref, wd_ref, o_ref,
                h_s, xg_s, og_s, *, gr):
    b = pl.program_id(0)
    e = pl.program_id(1)
    f = pl.program_id(2)
    n = cnt_ref[b * N_EXPERTS + e]
    o0 = off_ref[b * N_EXPERTS + e]

    @pl.when((e == 0) & (f == 0))
    def _():
        x = x_ref[...]
        h_s[...] = _rms(x, g_ref[...])
        o_ref[...] = x

    @pl.when((b == 0) & (e == 0) & (f == 0))
    def _():
        xg_s[...] = jnp.zeros_like(xg_s)

    def row_loop(body, unroll=4):
        def main(i, carry):
            for r in range(unroll):
                body(i * unroll + r)
            return carry

        def tail(i, carry):
            body(i)
            return carry
        lax.fori_loop(0, n // unroll, main, 0)
        lax.fori_loop((n // unroll) * unroll, n, tail, 0)

    @pl.when(f == 0)
    def _():
        def gather(i):
            xg_s[pl.ds(i, 1), :] = h_s[pl.ds(tok_ref[o0 + i], 1), :]
        row_loop(gather)

    def group(gi, carry):
        r0 = pl.multiple_of(gi * gr, gr)
        xb = xg_s[pl.ds(r0, gr), :].astype(BF16)
        a = jnp.dot(xb, wg_ref[...], preferred_element_type=F32)
        u = jnp.dot(xb, wu_ref[...], preferred_element_type=F32)
        y = jnp.dot((_silu(a) * u).astype(BF16), wd_ref[...], preferred_element_type=F32)

        @pl.when(f == 0)
        def _():
            og_s[pl.ds(r0, gr), :] = y

        @pl.when(f > 0)
        def _():
            og_s[pl.ds(r0, gr), :] += y
        return carry

    lax.fori_loop(0, (n + gr - 1) // gr, group, 0)

    @pl.when(f == pl.num_programs(2) - 1)
    def _():
        def scatter(i):
            tk = tok_ref[o0 + i]
            o_ref[pl.ds(tk, 1), :] += gts_ref[o0 + i] * og_s[pl.ds(i, 1), :]
        row_loop(scatter)


def _moe(x2, g, tok, off, cnt, gts, wg, wu, wd, tm, tf, gr):
    n = x2.shape[0]
    d_ff = wg.shape[2]
    kern = functools.partial(_moe_kernel, gr=gr)
    return pl.pallas_call(
        kern,
        grid_spec=pltpu.PrefetchScalarGridSpec(
            num_scalar_prefetch=3,
            grid=(n // tm, N_EXPERTS, d_ff // tf),
            in_specs=[pl.BlockSpec(memory_space=pltpu.SMEM),
                      pl.BlockSpec((tm, D_MODEL), lambda b, e, f, *_: (b, 0)),
                      pl.BlockSpec((1, D_MODEL), lambda b, e, f, *_: (0, 0)),
                      pl.BlockSpec((None, D_MODEL, tf), lambda b, e, f, *_: (e, 0, f)),
                      pl.BlockSpec((None, D_MODEL, tf), lambda b, e, f, *_: (e, 0, f)),
                      pl.BlockSpec((None, tf, D_MODEL), lambda b, e, f, *_: (e, f, 0))],
            out_specs=pl.BlockSpec((tm, D_MODEL), lambda b, e, f, *_: (b, 0)),
            scratch_shapes=[pltpu.VMEM((tm, D_MODEL), F32), pltpu.VMEM((-(-tm // gr) * gr, D_MODEL), F32),
                            pltpu.VMEM((-(-tm // gr) * gr, D_MODEL), F32)]),
        out_shape=jax.ShapeDtypeStruct((n, D_MODEL), F32),
        compiler_params=_cparams(("arbitrary", "arbitrary", "arbitrary")),
        name="moe",
    )(tok, off, cnt, gts, x2, g, wg, wu, wd)


def _moe_group_rows(tm):
    return min(tm, -(-(5 * tm // 16) // 64) * 64)


def _moe_plan(idx, gate, tm):
    n = idx.shape[1]
    nb = n // tm
    e = idx[:2].T.reshape(nb, 2 * tm)
    gt = gate[:2].T.reshape(nb, 2 * tm)
    order = jnp.argsort(e, axis=1, stable=True)
    tok = (order // 2).astype(I32)
    gts = jnp.take_along_axis(gt, order, axis=1)
    cnt = jnp.sum(e[:, :, None] == jnp.arange(N_EXPERTS, dtype=I32)[None, None, :], axis=1).astype(I32)
    off = jnp.cumsum(cnt, axis=1) - cnt + (jnp.arange(nb, dtype=I32) * (2 * tm))[:, None]
    return tok.reshape(-1), off.reshape(-1).astype(I32), cnt.reshape(-1), gts.reshape(-1)


def _layer(x, l, p, consts, past, cache_k, cache_v, page_table):
    b, seq, _ = x.shape
    n = b * seq
    is_sample = past is not None
    tm = min(n, 256)
    x2 = x.reshape(n, D_MODEL)
    bd, hm = consts["bd"], consts["hm"]

    qn, kn, v, z, xbc, dtr, pc, pd = _proj(x2, p["norm_mix"], p["w_in"], p["w_dt"], bd, p["gq"], p["gk"], tm)
    k_new = kn.reshape(b, seq, HEADS, HEAD_DIM)
    v_new = v.reshape(b, seq, HEADS, HEAD_DIM)

    if not is_sample:
        to_heads = lambda a: a.reshape(b, seq, HEADS, HEAD_DIM).transpose(0, 2, 1, 3)
        tq = min(seq, 512)
        tk = min(seq, 256)
        oa = _attn_prompt(to_heads(qn), to_heads(kn), to_heads(v), p["sb_bias"],
                          p["sb_out_norm"].reshape(HEADS, 1, HEAD_DIM), tq, tk)
        oa = oa.transpose(0, 2, 1, 3).reshape(n, GW)
    else:
        rows = HEADS * seq
        q16 = (qn.reshape(b, 1, seq, GW) * hm[None, :, None, :]).reshape(b, rows, GW)
        pad_keys = lambda a: jnp.pad(a.reshape(b, seq, GW), ((0, 0), (0, PAGE - seq), (0, 0)))
        o16 = _attn_sample(q16, jnp.repeat(p["sb_bias"], seq).reshape(rows, 1), cache_k, cache_v, l, page_table,
                           pad_keys(kn), pad_keys(v), jnp.repeat(hm, seq, axis=0),
                           p["sb_out_norm"].reshape(1, GW))
        oa = o16.reshape(b, HEADS, seq, GW).sum(axis=1).reshape(n, GW)

    lc = -(-seq // CHUNK) * CHUNK
    pad_seq = lambda a, to: jnp.pad(a.reshape(b, seq, -1), ((0, 0), (0, to - seq), (0, 0)))

    xbc3 = xbc.reshape(b, seq, SSM_CONV_DIM)
    if is_sample:
        conv0 = past["conv"]
        st0 = past["ssm"].reshape(b, GW, SSM_STATE).transpose(0, 2, 1)
    else:
        conv0 = jnp.zeros((b, SSM_CONV - 1, SSM_CONV_DIM), F32)
        st0 = jnp.zeros((b, SSM_STATE, GW), F32)
    conv8 = jnp.pad(conv0, ((0, 0), (SUBLANE - (SSM_CONV - 1), 0), (0, 0)))
    ob, st = _ssd(pad_seq(z, lc), pad_seq(xbc, lc), pad_seq(dtr, lc), conv8, st0,
                  p["conv_w"], p["conv_b"], p["dt_bias"], p["a_neg"], p["d_skip"], p["ssm_norm"], min(seq, CHUNK))
    ob = ob[:, :seq].reshape(n, GW)
    ssm_new = st.transpose(0, 2, 1).reshape(b, HEADS, HEAD_DIM, SSM_STATE)
    conv_new = jnp.concatenate([conv0, xbc3], axis=1)[:, seq:]

    pc3 = pc.reshape(b, seq, RWKV_PROJ)
    if is_sample:
        shift0 = past["shift"]
        wkv0 = past["wkv"].transpose(0, 2, 1, 3).reshape(b, HEAD_DIM, GW)
    else:
        shift0 = jnp.zeros((b, RWKV_PROJ), F32)
        wkv0 = jnp.zeros((b, HEAD_DIM, GW), F32)
    sh8 = jnp.pad(shift0[:, None, :], ((0, 0), (SUBLANE - 1, 0), (0, 0)))
    lr = -(-seq // HEAD_DIM) * HEAD_DIM
    oc, wkv = _rwkv(pad_seq(pc, lr), sh8, wkv0, p["shift_mu"], p["decay_w0"], p["iclr_a0"],
                    p["decay_up"], p["iclr_up"], p["gate_up"], p["k_k"], p["k_a"], p["r_k"], p["gn_w"], p["gn_b"],
                    bd, 2, min(seq, HEAD_DIM))
    oc = oc[:, :seq].reshape(n, GW)
    wkv_new = wkv.reshape(b, HEAD_DIM, HEADS, HEAD_DIM).transpose(0, 2, 1, 3)
    shift_new = pc3[:, -1]

    od, sgu_v = _sgu(pad_seq(pd, lc), p["sgu_w"], p["sgu_bias"], p["sgu_v_norm"], bd)
    od = od[:, :seq].reshape(n, GW)
    sgu_v = sgu_v[:, :seq].reshape(b, seq, HEADS, HEAD_DIM)

    x_mid = _outproj(x2, oa, ob, oc, od, p["w_out"], tm)

    if "ffn_wg" in p:
        tmf = min(n, 512)
        x_out = _ffn(x_mid, p["norm_ffn"], p["ffn_wg"], p["ffn_wu"], p["ffn_wd"], tmf, p["ffn_wg"].shape[1] // 2)
    else:
        tmm = min(n, 1024)
        idx, gate = _router(x_mid, p["norm_ffn"], p["router_wt"], min(n, 512))
        tok, off, cnt, gts = _moe_plan(idx, gate, tmm)
        x_out = _moe(x_mid, p["norm_ffn"], tok, off, cnt, gts, p["moe_wg"], p["moe_wu"], p["moe_wd"],
                     tmm, 512, _moe_group_rows(tmm))
    return x_out.reshape(b, seq, D_MODEL), (k_new, v_new, ssm_new, conv_new, wkv_new, shift_new, sgu_v)


def _trunk(x, layers, consts, pasts, cache_k, cache_v, page_table):
    states = []
    for l, p in enumerate(layers):
        x, st = _layer(x, l, p, consts, None if pasts is None else pasts[l], cache_k, cache_v, page_table)
        states.append(st)
    return x, [jnp.stack(s) for s in zip(*states)]


def _pad_rows(w, lo, total):
    return jnp.pad(w, ((lo, total - lo - w.shape[0]), (0, 0)))


def kernel(x_prompt, x_sample, cache_k, cache_v, page_table, state_ssm, state_conv, state_wkv, state_shift, norm_mix, norm_ffn, w_in, w_out, q_norm, k_norm, sb_bias, sb_out_norm, conv_w, conv_b, dt_bias, a_log, d_skip, ssm_norm, shift_mu, decay_w0, decay_up, iclr_a0, iclr_up, gate_up, k_k, k_a, r_k, gn_w, gn_b, sgu_w, sgu_b, sgu_v_norm, ffn_wg, ffn_wu, ffn_wd, router_w, moe_wg, moe_wu, moe_wd):
    depth = w_in.shape[0]
    lane = jnp.arange(GW)
    consts = {
        "bd": (lane[:, None] // HEAD_DIM == lane[None, :] // HEAD_DIM).astype(BF16),
        "hm": (jnp.arange(HEADS)[:, None] == lane[None, :] // HEAD_DIM).astype(F32),
    }
    dt_col = 4 * GW + SSM_CONV_DIM
    row = lambda a: a.reshape(1, -1).astype(F32)
    layers = []
    for l in range(depth):
        w = w_in[l]
        w_packed = jnp.concatenate(
            [w[:, :dt_col + HEADS], jnp.zeros((D_MODEL, LANE - HEADS), F32), w[:, dt_col + HEADS:]], axis=1)
        p = dict(
            norm_mix=row(norm_mix[l]), norm_ffn=row(norm_ffn[l]),
            w_in=w_packed.astype(BF16), w_dt=w_packed[:, dt_col:dt_col + LANE], w_out=w_out[l].astype(BF16),
            gq=row(jnp.tile(q_norm[l], HEADS)) * (HEAD_DIM ** -0.5), gk=row(jnp.tile(k_norm[l], HEADS)),
            sb_bias=sb_bias[l].astype(F32), sb_out_norm=sb_out_norm[l],
            conv_w=conv_w[l], conv_b=row(conv_b[l]),
            dt_bias=jnp.pad(row(dt_bias[l]), ((0, 0), (0, LANE - HEADS))),
            a_neg=jnp.pad(row(-jnp.exp(a_log[l])), ((0, 0), (0, LANE - HEADS))),
            d_skip=row(jnp.repeat(d_skip[l], HEAD_DIM)), ssm_norm=row(ssm_norm[l]),
            shift_mu=row(shift_mu[l]), decay_w0=row(decay_w0[l]), iclr_a0=row(iclr_a0[l]),
            decay_up=_pad_rows(decay_up[l], 0, RWKV_LORA_W).astype(BF16),
            iclr_up=_pad_rows(iclr_up[l], DECAY_LORA, RWKV_LORA_W).astype(BF16),
            gate_up=_pad_rows(gate_up[l], DECAY_LORA + ICLR_LORA, RWKV_LORA_W).astype(BF16),
            k_k=row(k_k[l]), k_a=row(k_a[l]), r_k=row(r_k[l]), gn_w=row(gn_w[l]), gn_b=row(gn_b[l]),
            sgu_w=sgu_w[l][:, :CHUNK, :CHUNK],
            sgu_bias=jnp.repeat(sgu_b[l][:, :CHUNK].T, HEAD_DIM, axis=1),
            sgu_v_norm=row(sgu_v_norm[l]),
        )
        if l % 2 == 0:
            p.update(ffn_wg=ffn_wg[l // 2].astype(BF16), ffn_wu=ffn_wu[l // 2].astype(BF16),
                     ffn_wd=ffn_wd[l // 2].astype(BF16))
        else:
            p.update(router_wt=router_w[l // 2].T, moe_wg=moe_wg[l // 2].astype(BF16),
                     moe_wu=moe_wu[l // 2].astype(BF16), moe_wd=moe_wd[l // 2].astype(BF16))
        layers.append(p)

    ck = cache_k.reshape(cache_k.shape[0], cache_k.shape[1], PAGE, GW)
    cv = cache_v.reshape(cache_v.shape[0], cache_v.shape[1], PAGE, GW)
    pasts = [dict(ssm=state_ssm[l], conv=state_conv[l], wkv=state_wkv[l], shift=state_shift[l])
             for l in range(depth)]

    y_p, (k_p, v_p, ssm_p, conv_p, wkv_p, shift_p, _) = _trunk(x_prompt, layers, consts, None, ck, cv, page_table)
    y_s, (k_s, v_s, ssm_s, conv_s, wkv_s, shift_s, sgu_v_s) = _trunk(x_sample, layers, consts, pasts, ck, cv, page_table)
    return (y_p, y_s, k_p, v_p, k_s, v_s, ssm_p, ssm_s, conv_p, conv_s, wkv_p, wkv_s, shift_p, shift_s, sgu_v_s)
```

```python
import functools

import jax
import jax.numpy as jnp
from jax import lax
from jax.experimental import pallas as pl
from jax.experimental.pallas import tpu as pltpu

F32, BF16, I32 = jnp.float32, jnp.bfloat16, jnp.int32

D_MODEL = 1024
HEAD_DIM = 64
HEADS = 4
GW = HEADS * HEAD_DIM
RMS_EPS = 1e-6
LOG2E = 1.4426950408889634
PAGE = 128
SSM_STATE = 128
SSM_GROUPS = 2
SSM_CONV = 4
SSM_CONV_DIM = GW + 2 * SSM_GROUPS * SSM_STATE
CHUNK = 128
RWKV_PROJ = 896
RWKV_LORA_LO = 3 * GW
RWKV_LORA_W = RWKV_PROJ - RWKV_LORA_LO
DECAY_LORA, ICLR_LORA, GATE_LORA = 32, 32, 64
RWKV_GN_EPS = HEAD_DIM * 1e-5
N_EXPERTS = 8
LANE = 128
SUBLANE = 8
VMEM_LIMIT = 52 * 1024 * 1024
ATTN_TQ, ATTN_TK = 512, 256
ATTN_GROUP = 2

_P_WIDTHS = (GW, GW, GW, GW, SSM_CONV_DIM, LANE, RWKV_PROJ, 2 * GW)
_P_TOTAL = sum(_P_WIDTHS)


def _cparams(sem):
    return pltpu.CompilerParams(dimension_semantics=sem, vmem_limit_bytes=VMEM_LIMIT)


def _const_spec(shape):
    nd = len(shape)
    return pl.BlockSpec(shape, lambda *_: (0,) * nd)


def _mm(a, b):
    return jnp.dot(a.astype(BF16), b.astype(BF16), preferred_element_type=F32)


def _mm_nt(a, b):
    return lax.dot_general(a.astype(BF16), b.astype(BF16), (((1,), (1,)), ((), ())),
                           preferred_element_type=F32)


def _parts(x, n):
    out, r = [], x
    for i in range(n):
        p = r.astype(BF16)
        out.append(p)
        if i + 1 < n:
            r = r - p.astype(F32)
    return out


def _mm_xl(a, b_exact, n):
    acc = None
    for p in _parts(a, n):
        t = jnp.dot(p, b_exact, preferred_element_type=F32)
        acc = t if acc is None else acc + t
    return acc


def _mm_xr(a_exact, b, n):
    acc = None
    for p in _parts(b, n):
        t = jnp.dot(a_exact, p, preferred_element_type=F32)
        acc = t if acc is None else acc + t
    return acc


def _seg_sum(x, bd):
    return _mm_xl(x, bd, 3)


def _sigmoid(x):
    return 1.0 / (1.0 + jnp.exp(-x))


def _silu(x):
    return x * _sigmoid(x)


def _softplus(x):
    return jnp.maximum(x, 0.0) + jnp.log1p(jnp.exp(-jnp.abs(x)))


def _gelu_tanh(x):
    return 0.5 * x * (1.0 + jnp.tanh(0.7978845608028654 * (x + 0.044715 * (x * x * x))))


def _rms(x, gain):
    return x * lax.rsqrt(jnp.mean(x * x, axis=-1, keepdims=True) + RMS_EPS) * gain


def _proj_kernel(x_ref, g_ref, w_ref, wdt_ref, bd_ref, gq_ref, gk_ref,
                 q_out, k_out, v_out, z_out, xbc_out, dt_out, c_out, d_out, *, attn_layout):
    h32 = _rms(x_ref[...], g_ref[...])
    h = h32.astype(BF16)
    bd = bd_ref[...]
    outs = (q_out, k_out, v_out, z_out, xbc_out, dt_out, c_out, d_out)
    lo = 0
    for idx, (width, out) in enumerate(zip(_P_WIDTHS, outs)):
        if out is dt_out:
            h_lo = (h32 - h.astype(F32)).astype(BF16)
            w_hi, w_lo = _parts(wdt_ref[...], 2)
            mm = lambda a, b: jnp.dot(a, b, preferred_element_type=F32)
            y = mm(h, w_hi) + (mm(h, w_lo) + mm(h_lo, w_hi))
        else:
            y = jnp.dot(h, w_ref[:, lo:lo + width], preferred_element_type=F32)
        if idx < 2:
            gain = (gq_ref, gk_ref)[idx][...]
            y = y * lax.rsqrt(_seg_sum(y * y, bd) * (1.0 / HEAD_DIM) + RMS_EPS) * gain
        if attn_layout and out is q_out:
            for hd in range(HEADS):
                q_out[0, hd] = y[:, hd * HEAD_DIM:(hd + 1) * HEAD_DIM]
        elif attn_layout and (out is k_out or out is v_out):
            out[0] = y.T
        else:
            out[...] = y
        lo += width


def _proj(x2, g, w, w_dt, bd, gq, gk, tm, seq, attn_layout):
    n = x2.shape[0]
    row = lambda wd: pl.BlockSpec((tm, wd), lambda i: (i, 0))
    out_specs = [row(wd) for wd in _P_WIDTHS]
    out_shape = [jax.ShapeDtypeStruct((n, wd), F32) for wd in _P_WIDTHS]
    if attn_layout:
        nb, per = n // seq, seq // tm
        out_specs[0] = pl.BlockSpec((1, HEADS, tm, HEAD_DIM), lambda i: (i // per, 0, i % per, 0))
        out_shape[0] = jax.ShapeDtypeStruct((nb, HEADS, seq, HEAD_DIM), F32)
        for j in (1, 2):
            out_specs[j] = pl.BlockSpec((1, GW, tm), lambda i: (i // per, 0, i % per))
            out_shape[j] = jax.ShapeDtypeStruct((nb, GW, seq), F32)
    return pl.pallas_call(
        functools.partial(_proj_kernel, attn_layout=attn_layout),
        grid=(n // tm,),
        in_specs=[row(D_MODEL),
                  _const_spec((1, D_MODEL)), _const_spec((D_MODEL, _P_TOTAL)), _const_spec((D_MODEL, LANE)),
                  _const_spec((GW, GW)), _const_spec((1, GW)), _const_spec((1, GW))],
        out_specs=out_specs,
        out_shape=out_shape,
        compiler_params=_cparams(("parallel",)),
        name="proj",
    )(x2, g, w, w_dt, bd, gq, gk)


def _sb_weights(neg_zs, readables, u, carry):
    lks, lbs = [], []
    for nz, readable in zip(neg_zs, readables):
        lk = jnp.minimum(nz, 0.0) - jnp.log2(1.0 + jnp.exp2(-jnp.abs(nz)))
        lbs.append(lk - nz)
        lks.append(lk if readable is None else jnp.where(readable, lk, 0.0))
    tails = [_mm_xl(lk, u, 2) for lk in lks]
    ws, scales = [], []
    for readable, lb, lk, tail in zip(readables, lbs, lks, tails):
        w = jnp.exp2(lb + tail)
        ws.append(w if readable is None else jnp.where(readable, w, 0.0))
        scales.append(jnp.exp2(carry))
        carry = carry + jnp.sum(lk, axis=1, keepdims=True)
    return ws, scales, carry


def _attn_prompt_kernel(bias_ref, q_ref, k_ref, v_ref, u_ref, gain_ref, o_ref, acc_ref, cs_ref, *, tq, tk, cg):
    h = pl.program_id(1)
    i = pl.program_id(2)
    neg_bias = bias_ref[h]
    u = u_ref[...]
    acc_ref[...] = jnp.zeros_like(acc_ref)
    cs_ref[...] = jnp.zeros_like(cs_ref)
    q = q_ref[0, 0].astype(BF16)
    n_diag = tq // tk
    key_minus_query = lax.broadcasted_iota(I32, (tq, tk), 1) - lax.broadcasted_iota(I32, (tq, tk), 0)

    def chunk_group(c_hi, n, masked):
        neg_zs, vs, readables = [], [], []
        for d in range(n):
            ks = pl.multiple_of((c_hi - d) * tk, tk)
            k_t = k_ref[0, :, pl.ds(ks, tk)].astype(BF16)
            vs.append(v_ref[0, :, pl.ds(ks, tk)].astype(BF16))
            neg_zs.append(jnp.dot(q, k_t, preferred_element_type=F32) + neg_bias)
            readables.append((key_minus_query < i * tq - ks) if masked else None)
        ws, scales, carry = _sb_weights(neg_zs, readables, u, cs_ref[...])
        acc = acc_ref[...]
        for w, sc, v_t in zip(ws, scales, vs):
            acc = acc + sc * lax.dot_general(w.astype(BF16), v_t, (((1,), (1,)), ((), ())),
                                             preferred_element_type=F32)
        acc_ref[...] = acc
        cs_ref[...] = carry

    chunk_group((i + 1) * n_diag - 1, n_diag, True)

    n_past = i * n_diag

    def past_body(j, carry):
        chunk_group(n_past - 1 - j * cg, cg, False)
        return carry

    lax.fori_loop(0, n_past // cg, past_body, 0)
    for rem in range(1, cg):
        @pl.when(n_past % cg == rem)
        def _():
            chunk_group(rem - 1, rem, False)
    o_ref[0, 0] = _rms(acc_ref[...], gain_ref[0])


def _attn_prompt(q, k, v, bias, gain, tq, tk, cg):
    b, h, l, d = q.shape
    u = (jnp.arange(tk)[:, None] > jnp.arange(tk)[None, :]).astype(BF16)
    kern = functools.partial(_attn_prompt_kernel, tq=tq, tk=tk, cg=cg)
    return pl.pallas_call(
        kern,
        grid=(b, h, l // tq),
        in_specs=[pl.BlockSpec(memory_space=pltpu.SMEM),
                  pl.BlockSpec((1, 1, tq, d), lambda bi, hi, qi: (bi, hi, qi, 0)),
                  pl.BlockSpec((1, d, l), lambda bi, hi, qi: (bi, hi, 0)),
                  pl.BlockSpec((1, d, l), lambda bi, hi, qi: (bi, hi, 0)),
                  _const_spec((tk, tk)),
                  pl.BlockSpec((1, 1, d), lambda bi, hi, qi: (hi, 0, 0))],
        out_specs=pl.BlockSpec((1, 1, tq, d), lambda bi, hi, qi: (bi, hi, qi, 0)),
        out_shape=jax.ShapeDtypeStruct((b, h, l, d), F32),
        scratch_shapes=[pltpu.VMEM((tq, d), F32), pltpu.VMEM((tq, 1), F32)],
        compiler_params=_cparams(("parallel", "parallel", "arbitrary")),
        name="attn_prompt",
    )(bias, q, k, v, u, gain)


def _attn_sample_kernel(pt_ref, q_ref, bias_ref, *refs, n_tok, n_group):
    kp_refs, vp_refs = refs[:n_group], refs[n_group:2 * n_group]
    kn_ref, vn_ref, u_ref, hm_ref, gain_ref, o_ref, acc_ref, cs_ref = refs[2 * n_group:]
    p = pl.program_id(1)
    rows = HEADS * n_tok
    q = q_ref[0].astype(BF16)

    def process(ks, vs, readables, acc, carry):
        neg_zs = [_mm(q, k_t) + bias_ref[...] for k_t in ks]
        ws, scales, carry = _sb_weights(neg_zs, readables, u_ref[...], carry)
        for w, sc, v_t in zip(ws, scales, vs):
            acc = acc + sc * _mm_nt(w, v_t)
        return acc, carry

    @pl.when(p == 0)
    def _():
        tok = lax.rem(lax.broadcasted_iota(I32, (rows, PAGE), 0), n_tok)
        key = lax.broadcasted_iota(I32, (rows, PAGE), 1)
        acc, carry = process([kn_ref[0]], [vn_ref[0]], [key < tok],
                             jnp.zeros(acc_ref.shape, F32), jnp.zeros(cs_ref.shape, F32))
        acc_ref[...] = acc
        cs_ref[...] = carry

    order = list(reversed(range(n_group)))
    acc, carry = process([kp_refs[j][...] for j in order], [vp_refs[j][...] for j in order], [None] * n_group,
                         acc_ref[...], cs_ref[...])
    acc_ref[...] = acc
    cs_ref[...] = carry

    @pl.when(p == pl.num_programs(1) - 1)
    def _():
        o = acc_ref[...] * hm_ref[...]
        ss = jnp.sum(o * o, axis=1, keepdims=True) * (1.0 / HEAD_DIM)
        o_ref[0] = o * lax.rsqrt(ss + RMS_EPS) * gain_ref[...]


def _attn_sample(q16, bias16, cache_k, cache_v, layer, page_table, k_new, v_new, hm16, gain):
    s, rows, _ = q16.shape
    n_pages = page_table.shape[1]
    u = (jnp.arange(PAGE)[:, None] > jnp.arange(PAGE)[None, :]).astype(BF16)

    n_group = next(g for g in (8, 4, 2, 1) if n_pages % g == 0)

    def page_spec(j):
        return pl.BlockSpec((None, None, GW, PAGE),
                            lambda si, p, pt: (layer, pt[si, n_pages - (p + 1) * n_group + j], 0, 0))

    page_specs = [page_spec(j) for j in range(n_group)]
    seq_spec = lambda shape: pl.BlockSpec(shape, lambda si, p, pt: (si, 0, 0))
    cst = lambda shape: pl.BlockSpec(shape, lambda si, p, pt: (0,) * len(shape))
    kern = functools.partial(_attn_sample_kernel, n_tok=rows // HEADS, n_group=n_group)
    return pl.pallas_call(
        kern,
        grid_spec=pltpu.PrefetchScalarGridSpec(
            num_scalar_prefetch=1,
            grid=(s, n_pages // n_group),
            in_specs=[seq_spec((1, rows, GW)), cst((rows, 1))] + page_specs + page_specs
            + [seq_spec((1, GW, PAGE)), seq_spec((1, GW, PAGE)),
               cst((PAGE, PAGE)), cst((rows, GW)), cst((1, GW))],
            out_specs=seq_spec((1, rows, GW)),
            scratch_shapes=[pltpu.VMEM((rows, GW), F32), pltpu.VMEM((rows, 1), F32)]),
        out_shape=jax.ShapeDtypeStruct((s, rows, GW), F32),
        compiler_params=_cparams(("parallel", "arbitrary")),
        name="attn_sample",
    )(page_table, q16, bias16, *([cache_k] * n_group), *([cache_v] * n_group), k_new, v_new, u, hm16, gain)


def _ssd_kernel(z_ref, x_ref, dt_ref, c0_ref, s0_ref, cw_ref, cb_ref, dtb_ref, a_ref, dsk_ref, nrm_ref, lt_ref,
                y_ref, st_out, xf_ref, st_ref, *, n_valid):
    t = CHUNK
    c = pl.program_id(1)

    @pl.when(c == 0)
    def _():
        xf_ref[0:SUBLANE, :] = c0_ref[0]
        st_ref[...] = s0_ref[0]

    xf_ref[SUBLANE:SUBLANE + t, :] = x_ref[0]
    conv = cb_ref[...]
    for i in range(SSM_CONV):
        conv = conv + cw_ref[i:i + 1, :] * xf_ref[pl.ds(SUBLANE - (SSM_CONV - 1) + i, t), :]
    xf_ref[0:SUBLANE, :] = xf_ref[t:t + SUBLANE, :]
    xc = _silu(conv)
    xs = xc[:, :GW]
    bm = xc[:, GW:GW + SSM_GROUPS * SSM_STATE]
    cm = xc[:, GW + SSM_GROUPS * SSM_STATE:]

    dt = _softplus(dt_ref[0] + dtb_ref[...])
    if n_valid < t:
        dt = jnp.where(lax.broadcasted_iota(I32, dt.shape, 0) < n_valid, dt, 0.0)
    acum = _mm_xr(lt_ref[...], dt * a_ref[...], 3)
    acum_t = acum.T
    dt_t = dt.T
    a_last = acum[t - 1:t, :]

    row = lax.broadcasted_iota(I32, (t, t), 0)
    col = lax.broadcasted_iota(I32, (t, t), 1)
    causal = row >= col
    lane = lax.broadcasted_iota(I32, (1, GW), 1)
    gmat = [_mm_nt(cm[:, g * SSM_STATE:(g + 1) * SSM_STATE], bm[:, g * SSM_STATE:(g + 1) * SSM_STATE])
            for g in range(SSM_GROUPS)]
    st_prev = st_ref[...]
    y = jnp.zeros((t, GW), F32)
    st_new = jnp.zeros((SSM_STATE, GW), F32)
    e_acum = jnp.zeros((t, GW), F32)
    a_end = jnp.zeros((1, GW), F32)
    for h in range(HEADS):
        g = h // (HEADS // SSM_GROUPS)
        mh = (lane // HEAD_DIM == h).astype(F32)
        col_h = acum[:, h:h + 1]
        decay = jnp.exp(jnp.where(causal, col_h - acum_t[h:h + 1, :], -1e30))
        scores = gmat[g] * decay * dt_t[h:h + 1, :]
        xm = xs * mh
        y = y + _mm(scores, xm)
        to_end = jnp.exp(a_last[:, h:h + 1] - col_h) * dt[:, h:h + 1]
        bw = bm[:, g * SSM_STATE:(g + 1) * SSM_STATE] * to_end
        st_new = st_new + _mm(bw.T, xm)
        e_acum = e_acum + mh * jnp.exp(col_h)
        a_end = a_end + mh * a_last[:, h:h + 1]
    y_in = jnp.where(lane < GW // SSM_GROUPS, _mm(cm[:, :SSM_STATE], st_prev), _mm(cm[:, SSM_STATE:], st_prev))
    st = st_prev * jnp.exp(a_end) + st_new
    st_ref[...] = st
    st_out[0] = st
    y = y + y_in * e_acum + dsk_ref[...] * xs
    y = y * _silu(z_ref[0])
    y_ref[0] = _rms(y, nrm_ref[...])


def _ssd(z, xbc, dt, conv0, st0, cw, cb, dtb, a_neg, dsk, nrm, n_valid):
    b, l, _ = z.shape
    lt = (jnp.arange(CHUNK)[:, None] >= jnp.arange(CHUNK)[None, :]).astype(BF16)
    blk = lambda wd: pl.BlockSpec((1, CHUNK, wd), lambda bi, ci: (bi, ci, 0))
    per_b = lambda shape: pl.BlockSpec(shape, lambda bi, ci: (bi, 0, 0))
    kern = functools.partial(_ssd_kernel, n_valid=n_valid)
    return pl.pallas_call(
        kern,
        grid=(b, l // CHUNK),
        in_specs=[blk(GW), blk(SSM_CONV_DIM), blk(LANE),
                  per_b((1, SUBLANE, SSM_CONV_DIM)), per_b((1, SSM_STATE, GW)),
                  _const_spec((SSM_CONV, SSM_CONV_DIM)), _const_spec((1, SSM_CONV_DIM)),
                  _const_spec((1, LANE)), _const_spec((1, LANE)), _const_spec((1, GW)), _const_spec((1, GW)),
                  _const_spec((CHUNK, CHUNK))],
        out_specs=[blk(GW), per_b((1, SSM_STATE, GW))],
        out_shape=[jax.ShapeDtypeStruct((b, l, GW), F32), jax.ShapeDtypeStruct((b, SSM_STATE, GW), F32)],
        scratch_shapes=[pltpu.VMEM((CHUNK + SUBLANE, SSM_CONV_DIM), F32), pltpu.VMEM((SSM_STATE, GW), F32)],
        compiler_params=_cparams(("parallel", "arbitrary")),
        name="ssd",
    )(z, xbc, dt, conv0, st0, cw, cb, dtb, a_neg, dsk, nrm, lt)


def _rwkv_kernel(p_ref, sh0_ref, s0_ref, mu_ref, w0_ref, a0_ref, dup_ref, iup_ref, gup_ref,
                 kk_ref, ka_ref, rk_ref, gnw_ref, gnb_ref, bd_ref, lt_ref,
                 y_ref, s_out, pf_ref, st_ref, *, bb, t, n_valid):
    c = pl.program_id(1)
    bd = bd_ref[...]
    lane = lax.broadcasted_iota(I32, (1, GW), 1)
    head_mask = [(lane // HEAD_DIM == h).astype(F32) for h in range(HEADS)]
    stack = lambda x: jnp.concatenate([x * m for m in head_mask], axis=0)
    unstack = lambda x: sum(x[h * t:(h + 1) * t] for h in range(HEADS))
    step_r = lax.broadcasted_iota(I32, (HEADS * t, HEADS * t), 0) % t
    step_c = lax.broadcasted_iota(I32, (HEADS * t, HEADS * t), 1) % t
    nt = lambda x, y: lax.dot_general(x, y, (((1,), (1,)), ((), ())), preferred_element_type=F32)
    tn = lambda x, y: lax.dot_general(x, y, (((0,), (0,)), ((), ())), preferred_element_type=F32)
    mm = lambda x, y: jnp.dot(x, y, preferred_element_type=F32)

    @pl.when(c == 0)
    def _():
        for b in range(bb):
            pf_ref[b, 0:SUBLANE, :] = sh0_ref[b]
            st_ref[b] = stack(s0_ref[b])

    for b in range(bb):
        pf_ref[b, SUBLANE:SUBLANE + t, :] = p_ref[b]
        cur = p_ref[b]
        prev = pf_ref[b, pl.ds(SUBLANE - 1, t), :]
        pf_ref[b, 0:SUBLANE, :] = pf_ref[b, t:t + SUBLANE, :]
        xs = cur + (prev - cur) * mu_ref[...]
        r = xs[:, 0:GW]
        k = xs[:, GW:2 * GW]
        v = xs[:, 2 * GW:3 * GW]
        lo = xs[:, RWKV_LORA_LO:RWKV_PROJ]
        w_log = -_softplus(-(w0_ref[...] + _mm(jnp.tanh(lo), dup_ref[...]))) - 0.5
        a = _sigmoid(a0_ref[...] + _mm(lo, iup_ref[...]))
        kkr = k * kk_ref[...]
        kk = kkr / jnp.maximum(jnp.sqrt(_seg_sum(kkr * kkr, bd)), 1e-12)
        kmod = k * (1.0 + (a - 1.0) * ka_ref[...])
        ka = kk * a
        log_w = -jnp.exp(w_log)
        if n_valid < t:
            valid = lax.broadcasted_iota(I32, (t, GW), 0) < n_valid
            log_w = jnp.where(valid, log_w, 0.0)
            kk, ka, kmod, v = (jnp.where(valid, x, 0.0) for x in (kk, ka, kmod, v))
        bonus = _seg_sum(r * kmod * rk_ref[...], bd) * v
        gate = _mm(_sigmoid(lo), gup_ref[...])

        cum = _mm_xr(lt_ref[...], log_w, 3)
        inv_g = jnp.exp(-cum)
        a_s = stack(kk * jnp.exp(cum - log_w)).astype(BF16)
        b_s = stack(ka * inv_g).astype(BF16)
        k_s = stack(kmod * inv_g).astype(BF16)
        r_s = stack(r * jnp.exp(cum)).astype(BF16)
        v_s = stack(v).astype(BF16)
        s0 = st_ref[b]
        s0_b = s0.astype(BF16)

        strict = step_r > step_c
        incl = step_r >= step_c
        neg_l = jnp.where(strict, -nt(a_s, b_s), 0.0).astype(BF16)
        l_ak = jnp.where(strict, nt(a_s, k_s), 0.0).astype(BF16)
        l_rk = jnp.where(incl, nt(r_s, k_s), 0.0).astype(BF16)
        l_rb = jnp.where(incl, nt(r_s, b_s), 0.0).astype(BF16)
        u = nt(a_s, s0_b) + mm(l_ak, v_s)
        power = neg_l
        u = u + mm(power, u.astype(BF16))
        for _ in range(t.bit_length() - 2):
            power = mm(power, power).astype(BF16)
            u = u + mm(power, u.astype(BF16))
        u_b = u.astype(BF16)
        y = unstack(nt(r_s, s0_b) + mm(l_rk, v_s) - mm(l_rb, u_b))
        st_ref[b] = (s0 + tn(v_s, k_s) - tn(u_b, b_s)) * jnp.exp(cum[t - 1:t, :])

        mean = _seg_sum(y, bd) * (1.0 / HEAD_DIM)
        yc = y - mean
        var = _seg_sum(yc * yc, bd) * (1.0 / HEAD_DIM)
        yn = yc * lax.rsqrt(var + RWKV_GN_EPS) * gnw_ref[...] + gnb_ref[...]
        y_ref[b] = (yn + bonus) * gate
        s_out[b] = unstack(st_ref[b])


def _rwkv(pc, sh0, s0, mu, w0, a0, dup, iup, gup, k_k, k_a, r_k, gnw, gnb, bd, bb, n_valid):
    b, l, _ = pc.shape
    t = HEAD_DIM
    lt = (jnp.arange(t)[:, None] >= jnp.arange(t)[None, :]).astype(BF16)
    blk = lambda wd: pl.BlockSpec((bb, t, wd), lambda bi, ci: (bi, ci, 0))
    per_b = lambda shape: pl.BlockSpec(shape, lambda bi, ci: (bi, 0, 0))
    vec = lambda wd: _const_spec((1, wd))
    kern = functools.partial(_rwkv_kernel, bb=bb, t=t, n_valid=n_valid)
    return pl.pallas_call(
        kern,
        grid=(b // bb, l // t),
        in_specs=[blk(RWKV_PROJ), per_b((bb, SUBLANE, RWKV_PROJ)), per_b((bb, HEAD_DIM, GW)),
                  vec(RWKV_PROJ), vec(GW), vec(GW),
                  _const_spec((RWKV_LORA_W, GW)), _const_spec((RWKV_LORA_W, GW)), _const_spec((RWKV_LORA_W, GW)),
                  vec(GW), vec(GW), vec(GW), vec(GW), vec(GW),
                  _const_spec((GW, GW)), _const_spec((t, t))],
        out_specs=[blk(GW), per_b((bb, HEAD_DIM, GW))],
        out_shape=[jax.ShapeDtypeStruct((b, l, GW), F32), jax.ShapeDtypeStruct((b, HEAD_DIM, GW), F32)],
        scratch_shapes=[pltpu.VMEM((bb, t + SUBLANE, RWKV_PROJ), F32), pltpu.VMEM((bb, HEADS * HEAD_DIM, GW), F32)],
        compiler_params=_cparams(("parallel", "arbitrary")),
        name="rwkv",
    )(pc, sh0, s0, mu, w0, a0, dup, iup, gup, k_k, k_a, r_k, gnw, gnb, bd, lt)


def _sgu_kernel(p_ref, w_ref, bias_ref, gv_ref, bd_ref, o_ref, v_out):
    t = CHUNK
    p = p_ref[0]
    u = _gelu_tanh(p[:, :GW])
    v = _gelu_tanh(p[:, GW:])
    vn = v * lax.rsqrt(_seg_sum(v * v, bd_ref[...]) * (1.0 / HEAD_DIM) + RMS_EPS) * gv_ref[...]
    v_out[0] = vn
    causal = lax.broadcasted_iota(I32, (t, t), 0) >= lax.broadcasted_iota(I32, (t, t), 1)
    lane = lax.broadcasted_iota(I32, (1, GW), 1)
    mixed = bias_ref[...]
    for g in range(HEADS):
        wg = jnp.where(causal, w_ref[g], 0.0)
        mixed = mixed + _mm(wg, vn * (lane // HEAD_DIM == g).astype(F32))
    o_ref[0] = u * mixed


def _sgu(pd, w, bias, gv, bd):
    b, l, _ = pd.shape
    return pl.pallas_call(
        _sgu_kernel,
        grid=(b, l // CHUNK),
        in_specs=[pl.BlockSpec((1, CHUNK, 2 * GW), lambda bi, ci: (bi, ci, 0)),
                  _const_spec((HEADS, CHUNK, CHUNK)), _const_spec((CHUNK, GW)), _const_spec((1, GW)),
                  _const_spec((GW, GW))],
        out_specs=[pl.BlockSpec((1, CHUNK, GW), lambda bi, ci: (bi, ci, 0))] * 2,
        out_shape=[jax.ShapeDtypeStruct((b, l, GW), F32)] * 2,
        compiler_params=_cparams(("parallel", "parallel")),
        name="sgu",
    )(pd, w, bias, gv, bd)


def _outproj_kernel(x_ref, oa_ref, ob_ref, oc_ref, od_ref, w_ref, o_ref, *, oa_heads):
    acc = x_ref[...]
    if oa_heads:
        for hd in range(HEADS):
            acc = acc + jnp.dot(oa_ref[0, hd].astype(BF16), w_ref[hd * HEAD_DIM:(hd + 1) * HEAD_DIM, :],
                                preferred_element_type=F32)
    for i, r in enumerate((oa_ref, ob_ref, oc_ref, od_ref)):
        if i > 0 or not oa_heads:
            acc = acc + jnp.dot(r[...].astype(BF16), w_ref[i * GW:(i + 1) * GW, :], preferred_element_type=F32)
    o_ref[...] = acc


def _outproj(x2, oa, ob, oc, od, w, tm):
    n = x2.shape[0]
    row = lambda wd: pl.BlockSpec((tm, wd), lambda i: (i, 0))
    oa_heads = oa.ndim == 4
    if oa_heads:
        per = oa.shape[2] // tm
        oa_spec = pl.BlockSpec((1, HEADS, tm, HEAD_DIM), lambda i: (i // per, 0, i % per, 0))
    else:
        oa_spec = row(GW)
    return pl.pallas_call(
        functools.partial(_outproj_kernel, oa_heads=oa_heads),
        grid=(n // tm,),
        in_specs=[row(D_MODEL), oa_spec, row(GW), row(GW), row(GW), _const_spec((4 * GW, D_MODEL))],
        out_specs=row(D_MODEL),
        out_shape=jax.ShapeDtypeStruct((n, D_MODEL), F32),
        compiler_params=_cparams(("parallel",)),
        name="outproj",
    )(x2, oa, ob, oc, od, w)


def _ffn_kernel(x_ref, g_ref, wg_ref, wu_ref, wd_ref, o_ref, h_s):
    f = pl.program_id(1)

    @pl.when(f == 0)
    def _():
        x = x_ref[...]
        h_s[...] = _rms(x, g_ref[...]).astype(BF16)
        o_ref[...] = x

    h = h_s[...]
    a = jnp.dot(h, wg_ref[...], preferred_element_type=F32)
    u = jnp.dot(h, wu_ref[...], preferred_element_type=F32)
    o_ref[...] += jnp.dot((_silu(a) * u).astype(BF16), wd_ref[...], preferred_element_type=F32)


def _ffn(x2, g, wg, wu, wd, tm, tf):
    n = x2.shape[0]
    d_ff = wg.shape[1]
    return pl.pallas_call(
        _ffn_kernel,
        grid=(n // tm, d_ff // tf),
        in_specs=[pl.BlockSpec((tm, D_MODEL), lambda i, f: (i, 0)), _const_spec((1, D_MODEL)),
                  pl.BlockSpec((D_MODEL, tf), lambda i, f: (0, f)),
                  pl.BlockSpec((D_MODEL, tf), lambda i, f: (0, f)),
                  pl.BlockSpec((tf, D_MODEL), lambda i, f: (f, 0))],
        out_specs=pl.BlockSpec((tm, D_MODEL), lambda i, f: (i, 0)),
        out_shape=jax.ShapeDtypeStruct((n, D_MODEL), F32),
        scratch_shapes=[pltpu.VMEM((tm, D_MODEL), BF16)],
        compiler_params=_cparams(("parallel", "arbitrary")),
        name="ffn",
    )(x2, g, wg, wu, wd)


def _router_kernel(x_ref, g_ref, rwt_ref, idx_ref, gate_ref):
    h = _rms(x_ref[...], g_ref[...])
    hh, hl = _parts(h, 2)
    wh, wl = _parts(rwt_ref[...], 2)
    nt = lambda a, b: lax.dot_general(a, b, (((1,), (1,)), ((), ())), preferred_element_type=F32)
    logits = nt(wh, hh) + (nt(wh, hl) + nt(wl, hh))
    e_id = lax.broadcasted_iota(I32, logits.shape, 0)
    m1 = jnp.max(logits, axis=0, keepdims=True)
    i1 = jnp.min(jnp.where(logits == m1, e_id, N_EXPERTS), axis=0, keepdims=True)
    rest = jnp.where(e_id == i1, -jnp.inf, logits)
    m2 = jnp.max(rest, axis=0, keepdims=True)
    i2 = jnp.min(jnp.where(rest == m2, e_id, N_EXPERTS), axis=0, keepdims=True)
    e = jnp.exp(m2 - m1)
    g1 = 1.0 / (1.0 + e)
    idx_ref[...] = jnp.where(e_id == 0, i1, jnp.where(e_id == 1, i2, 0))
    gate_ref[...] = jnp.where(e_id == 0, g1, jnp.where(e_id == 1, e * g1, 0.0))


def _router(x2, g, rwt, tm):
    n = x2.shape[0]
    return pl.pallas_call(
        _router_kernel,
        grid=(n // tm,),
        in_specs=[pl.BlockSpec((tm, D_MODEL), lambda i: (i, 0)), _const_spec((1, D_MODEL)),
                  _const_spec((N_EXPERTS, D_MODEL))],
        out_specs=[pl.BlockSpec((N_EXPERTS, tm), lambda i: (0, i))] * 2,
        out_shape=[jax.ShapeDtypeStruct((N_EXPERTS, n), I32), jax.ShapeDtypeStruct((N_EXPERTS, n), F32)],
        compiler_params=_cparams(("parallel",)),
        name="router",
    )(x2, g, rwt)


def _moe_kernel(tok_ref, off_ref, cnt_ref, gts_ref, x_ref, g_ref, wg_ref, wu_ref, wd_ref, o_ref,
                h_s, xg_s, og_s, *, gr):
    b = pl.program_id(0)
    e = pl.program_id(1)
    f = pl.program_id(2)
    n = cnt_ref[b * N_EXPERTS + e]
    o0 = off_ref[b * N_EXPERTS + e]

    @pl.when((e == 0) & (f == 0))
    def _():
        x = x_ref[...]
        h_s[...] = _rms(x, g_ref[...])
        o_ref[...] = x

    @pl.when((b == 0) & (e == 0) & (f == 0))
    def _():
        xg_s[...] = jnp.zeros_like(xg_s)

    def row_loop(body, unroll=4):
        def main(i, carry):
            for r in range(unroll):
                body(i * unroll + r)
            return carry

        def tail(i, carry):
            body(i)
            return carry
        lax.fori_loop(0, n // unroll, main, 0)
        lax.fori_loop((n // unroll) * unroll, n, tail, 0)

    @pl.when(f == 0)
    def _():
        def gather(i):
            xg_s[pl.ds(i, 1), :] = h_s[pl.ds(tok_ref[o0 + i], 1), :]
        row_loop(gather)

    def group(gi, carry):
        r0 = pl.multiple_of(gi * gr, gr)
        xb = xg_s[pl.ds(r0, gr), :].astype(BF16)
        a = jnp.dot(xb, wg_ref[...], preferred_element_type=F32)
        u = jnp.dot(xb, wu_ref[...], preferred_element_type=F32)
        y = jnp.dot((_silu(a) * u).astype(BF16), wd_ref[...], preferred_element_type=F32)

        @pl.when(f == 0)
        def _():
            og_s[pl.ds(r0, gr), :] = y

        @pl.when(f > 0)
        def _():
            og_s[pl.ds(r0, gr), :] += y
        return carry

    lax.fori_loop(0, (n + gr - 1) // gr, group, 0)

    @pl.when(f == pl.num_programs(2) - 1)
    def _():
        def scatter(i):
            tk = tok_ref[o0 + i]
            o_ref[pl.ds(tk, 1), :] += gts_ref[o0 + i] * og_s[pl.ds(i, 1), :]
        row_loop(scatter)


def _moe(x2, g, tok, off, cnt, gts, wg, wu, wd, tm, tf, gr):
    n = x2.shape[0]
    d_ff = wg.shape[2]
    kern = functools.partial(_moe_kernel, gr=gr)
    return pl.pallas_call(
        kern,
        grid_spec=pltpu.PrefetchScalarGridSpec(
            num_scalar_prefetch=3,
            grid=(n // tm, N_EXPERTS, d_ff // tf),
            in_specs=[pl.BlockSpec(memory_space=pltpu.SMEM),
                      pl.BlockSpec((tm, D_MODEL), lambda b, e, f, *_: (b, 0)),
                      pl.BlockSpec((1, D_MODEL), lambda b, e, f, *_: (0, 0)),
                      pl.BlockSpec((None, D_MODEL, tf), lambda b, e, f, *_: (e, 0, f)),
                      pl.BlockSpec((None, D_MODEL, tf), lambda b, e, f, *_: (e, 0, f)),
                      pl.BlockSpec((None, tf, D_MODEL), lambda b, e, f, *_: (e, f, 0))],
            out_specs=pl.BlockSpec((tm, D_MODEL), lambda b, e, f, *_: (b, 0)),
            scratch_shapes=[pltpu.VMEM((tm, D_MODEL), F32), pltpu.VMEM((-(-tm // gr) * gr, D_MODEL), F32),
                            pltpu.VMEM((-(-tm // gr) * gr, D_MODEL), F32)]),
        out_shape=jax.ShapeDtypeStruct((n, D_MODEL), F32),
        compiler_params=_cparams(("arbitrary", "arbitrary", "arbitrary")),
        name="moe",
    )(tok, off, cnt, gts, x2, g, wg, wu, wd)


def _moe_group_rows(tm):
    return min(tm, -(-(5 * tm // 16) // 64) * 64)


def _moe_plan(idx, gate, tm):
    n = idx.shape[1]
    nb = n // tm
    e = idx[:2].T.reshape(nb, 2 * tm)
    gt = gate[:2].T.reshape(nb, 2 * tm)
    order = jnp.argsort(e, axis=1, stable=True)
    tok = (order // 2).astype(I32)
    gts = jnp.take_along_axis(gt, order, axis=1)
    cnt = jnp.sum(e[:, :, None] == jnp.arange(N_EXPERTS, dtype=I32)[None, None, :], axis=1).astype(I32)
    off = jnp.cumsum(cnt, axis=1) - cnt + (jnp.arange(nb, dtype=I32) * (2 * tm))[:, None]
    return tok.reshape(-1), off.reshape(-1).astype(I32), cnt.reshape(-1), gts.reshape(-1)


def _layer(x, l, p, consts, past, cache_k, cache_v, page_table):
    b, seq, _ = x.shape
    n = b * seq
    is_sample = past is not None
    tm = min(n, 256)
    x2 = x.reshape(n, D_MODEL)
    bd, hm = consts["bd"], consts["hm"]

    qn, kn, v, z, xbc, dtr, pc, pd = _proj(x2, p["norm_mix"], p["w_in"], p["w_dt"], bd, p["gq"], p["gk"], tm,
                                           seq, not is_sample)

    if not is_sample:
        from_feature_major = lambda a: a.reshape(b, HEADS, HEAD_DIM, seq).transpose(0, 3, 1, 2)
        k_new, v_new = from_feature_major(kn), from_feature_major(v)
        oa = _attn_prompt(qn, kn, v, p["sb_bias"], p["sb_out_norm"].reshape(HEADS, 1, HEAD_DIM),
                          min(seq, ATTN_TQ), min(seq, ATTN_TK), ATTN_GROUP)
    else:
        k_new = kn.reshape(b, seq, HEADS, HEAD_DIM)
        v_new = v.reshape(b, seq, HEADS, HEAD_DIM)
        rows = HEADS * seq
        q16 = (qn.reshape(b, 1, seq, GW) * hm[None, :, None, :]).reshape(b, rows, GW)
        pad_keys_t = lambda a: jnp.pad(a.reshape(b, seq, GW), ((0, 0), (0, PAGE - seq), (0, 0))).transpose(0, 2, 1)
        o16 = _attn_sample(q16, jnp.repeat(p["sb_bias"], seq).reshape(rows, 1), cache_k, cache_v, l, page_table,
                           pad_keys_t(kn), pad_keys_t(v), jnp.repeat(hm, seq, axis=0),
                           p["sb_out_norm"].reshape(1, GW))
        oa = o16.reshape(b, HEADS, seq, GW).sum(axis=1).reshape(n, GW)

    lc = -(-seq // CHUNK) * CHUNK
    pad_seq = lambda a, to: jnp.pad(a.reshape(b, seq, -1), ((0, 0), (0, to - seq), (0, 0)))

    xbc3 = xbc.reshape(b, seq, SSM_CONV_DIM)
    if is_sample:
        conv0 = past["conv"]
        st0 = past["ssm"].reshape(b, GW, SSM_STATE).transpose(0, 2, 1)
    else:
        conv0 = jnp.zeros((b, SSM_CONV - 1, SSM_CONV_DIM), F32)
        st0 = jnp.zeros((b, SSM_STATE, GW), F32)
    conv8 = jnp.pad(conv0, ((0, 0), (SUBLANE - (SSM_CONV - 1), 0), (0, 0)))
    ob, st = _ssd(pad_seq(z, lc), pad_seq(xbc, lc), pad_seq(dtr, lc), conv8, st0,
                  p["conv_w"], p["conv_b"], p["dt_bias"], p["a_neg"], p["d_skip"], p["ssm_norm"], min(seq, CHUNK))
    ob = ob[:, :seq].reshape(n, GW)
    ssm_new = st.transpose(0, 2, 1).reshape(b, HEADS, HEAD_DIM, SSM_STATE)
    conv_new = jnp.concatenate([conv0, xbc3], axis=1)[:, seq:]

    pc3 = pc.reshape(b, seq, RWKV_PROJ)
    if is_sample:
        shift0 = past["shift"]
        wkv0 = past["wkv"].transpose(0, 2, 1, 3).reshape(b, HEAD_DIM, GW)
    else:
        shift0 = jnp.zeros((b, RWKV_PROJ), F32)
        wkv0 = jnp.zeros((b, HEAD_DIM, GW), F32)
    sh8 = jnp.pad(shift0[:, None, :], ((0, 0), (SUBLANE - 1, 0), (0, 0)))
    lr = -(-seq // HEAD_DIM) * HEAD_DIM
    oc, wkv = _rwkv(pad_seq(pc, lr), sh8, wkv0, p["shift_mu"], p["decay_w0"], p["iclr_a0"],
                    p["decay_up"], p["iclr_up"], p["gate_up"], p["k_k"], p["k_a"], p["r_k"], p["gn_w"], p["gn_b"],
                    bd, 2, min(seq, HEAD_DIM))
    oc = oc[:, :seq].reshape(n, GW)
    wkv_new = wkv.reshape(b, HEAD_DIM, HEADS, HEAD_DIM).transpose(0, 2, 1, 3)
    shift_new = pc3[:, -1]

    od, sgu_v = _sgu(pad_seq(pd, lc), p["sgu_w"], p["sgu_bias"], p["sgu_v_norm"], bd)
    od = od[:, :seq].reshape(n, GW)
    sgu_v = sgu_v[:, :seq].reshape(b, seq, HEADS, HEAD_DIM)

    x_mid = _outproj(x2, oa, ob, oc, od, p["w_out"], tm)

    if "ffn_wg" in p:
        tmf = min(n, 512)
        x_out = _ffn(x_mid, p["norm_ffn"], p["ffn_wg"], p["ffn_wu"], p["ffn_wd"], tmf, p["ffn_wg"].shape[1] // 2)
    else:
        tmm = min(n, 1024)
        idx, gate = _router(x_mid, p["norm_ffn"], p["router_wt"], min(n, 512))
        tok, off, cnt, gts = _moe_plan(idx, gate, tmm)
        x_out = _moe(x_mid, p["norm_ffn"], tok, off, cnt, gts, p["moe_wg"], p["moe_wu"], p["moe_wd"],
                     tmm, 896, _moe_group_rows(tmm))
    return x_out.reshape(b, seq, D_MODEL), (k_new, v_new, ssm_new, conv_new, wkv_new, shift_new, sgu_v)


def _trunk(x, layers, consts, pasts, cache_k, cache_v, page_table):
    states = []
    for l, p in enumerate(layers):
        x, st = _layer(x, l, p, consts, None if pasts is None else pasts[l], cache_k, cache_v, page_table)
        states.append(st)
    return x, [jnp.stack(s) for s in zip(*states)]


def _pad_rows(w, lo, total):
    return jnp.pad(w, ((lo, total - lo - w.shape[0]), (0, 0)))


def kernel(x_prompt, x_sample, cache_k, cache_v, page_table, state_ssm, state_conv, state_wkv, state_shift, norm_mix, norm_ffn, w_in, w_out, q_norm, k_norm, sb_bias, sb_out_norm, conv_w, conv_b, dt_bias, a_log, d_skip, ssm_norm, shift_mu, decay_w0, decay_up, iclr_a0, iclr_up, gate_up, k_k, k_a, r_k, gn_w, gn_b, sgu_w, sgu_b, sgu_v_norm, ffn_wg, ffn_wu, ffn_wd, router_w, moe_wg, moe_wu, moe_wd):
    depth = w_in.shape[0]
    lane = jnp.arange(GW)
    consts = {
        "bd": (lane[:, None] // HEAD_DIM == lane[None, :] // HEAD_DIM).astype(BF16),
        "hm": (jnp.arange(HEADS)[:, None] == lane[None, :] // HEAD_DIM).astype(F32),
    }
    dt_col = 4 * GW + SSM_CONV_DIM
    row = lambda a: a.reshape(1, -1).astype(F32)
    layers = []
    for l in range(depth):
        w = w_in[l]
        w_packed = jnp.concatenate(
            [w[:, :dt_col + HEADS], jnp.zeros((D_MODEL, LANE - HEADS), F32), w[:, dt_col + HEADS:]], axis=1)
        p = dict(
            norm_mix=row(norm_mix[l]), norm_ffn=row(norm_ffn[l]),
            w_in=w_packed.astype(BF16), w_dt=w_packed[:, dt_col:dt_col + LANE], w_out=w_out[l].astype(BF16),
            gq=row(jnp.tile(q_norm[l], HEADS)) * -(LOG2E * HEAD_DIM ** -0.5), gk=row(jnp.tile(k_norm[l], HEADS)),
            sb_bias=sb_bias[l].astype(F32) * -LOG2E, sb_out_norm=sb_out_norm[l],
            conv_w=conv_w[l], conv_b=row(conv_b[l]),
            dt_bias=jnp.pad(row(dt_bias[l]), ((0, 0), (0, LANE - HEADS))),
            a_neg=jnp.pad(row(-jnp.exp(a_log[l])), ((0, 0), (0, LANE - HEADS))),
            d_skip=row(jnp.repeat(d_skip[l], HEAD_DIM)), ssm_norm=row(ssm_norm[l]),
            shift_mu=row(shift_mu[l]), decay_w0=row(decay_w0[l]), iclr_a0=row(iclr_a0[l]),
            decay_up=_pad_rows(decay_up[l], 0, RWKV_LORA_W).astype(BF16),
            iclr_up=_pad_rows(iclr_up[l], DECAY_LORA, RWKV_LORA_W).astype(BF16),
            gate_up=_pad_rows(gate_up[l], DECAY_LORA + ICLR_LORA, RWKV_LORA_W).astype(BF16),
            k_k=row(k_k[l]), k_a=row(k_a[l]), r_k=row(r_k[l]), gn_w=row(gn_w[l]), gn_b=row(gn_b[l]),
            sgu_w=sgu_w[l][:, :CHUNK, :CHUNK],
            sgu_bias=jnp.repeat(sgu_b[l][:, :CHUNK].T, HEAD_DIM, axis=1),
            sgu_v_norm=row(sgu_v_norm[l]),
        )
        if l % 2 == 0:
            p.update(ffn_wg=ffn_wg[l // 2].astype(BF16), ffn_wu=ffn_wu[l // 2].astype(BF16),
                     ffn_wd=ffn_wd[l // 2].astype(BF16))
        else:
            p.update(router_wt=router_w[l // 2].T, moe_wg=moe_wg[l // 2].astype(BF16),
                     moe_wu=moe_wu[l // 2].astype(BF16), moe_wd=moe_wd[l // 2].astype(BF16))
        layers.append(p)

    to_pages_t = lambda c: c.transpose(0, 1, 3, 4, 2).reshape(c.shape[0], c.shape[1], GW, PAGE)
    ck, cv = to_pages_t(cache_k), to_pages_t(cache_v)
    pasts = [dict(ssm=state_ssm[l], conv=state_conv[l], wkv=state_wkv[l], shift=state_shift[l])
             for l in range(depth)]

    y_p, (k_p, v_p, ssm_p, conv_p, wkv_p, shift_p, _) = _trunk(x_prompt, layers, consts, None, ck, cv, page_table)
    y_s, (k_s, v_s, ssm_s, conv_s, wkv_s, shift_s, sgu_v_s) = _trunk(x_sample, layers, consts, pasts, ck, cv, page_table)
    return (y_p, y_s, k_p, v_p, k_s, v_s, ssm_p, ssm_s, conv_p, conv_s, wkv_p, wkv_s, shift_p, shift_s, sgu_v_s)
```

```python
import functools

import jax
import jax.numpy as jnp
from jax import lax
from jax.experimental import pallas as pl
from jax.experimental.pallas import tpu as pltpu

F32, BF16, I32 = jnp.float32, jnp.bfloat16, jnp.int32

D_MODEL = 1024
HEAD_DIM = 64
HEADS = 4
GW = HEADS * HEAD_DIM
RMS_EPS = 1e-6
LOG2E = 1.4426950408889634
PAGE = 128
SSM_STATE = 128
SSM_GROUPS = 2
SSM_CONV = 4
SSM_CONV_DIM = GW + 2 * SSM_GROUPS * SSM_STATE
CHUNK = 128
RWKV_PROJ = 896
RWKV_LORA_LO = 3 * GW
RWKV_LORA_W = RWKV_PROJ - RWKV_LORA_LO
DECAY_LORA, ICLR_LORA, GATE_LORA = 32, 32, 64
RWKV_GN_EPS = HEAD_DIM * 1e-5
N_EXPERTS = 8
LANE = 128
SUBLANE = 8
VMEM_LIMIT = 52 * 1024 * 1024
ATTN_TQ, ATTN_TK = 512, 256

_P_WIDTHS = (GW, GW, GW, GW, SSM_CONV_DIM, LANE, RWKV_PROJ, 2 * GW)
_P_TOTAL = sum(_P_WIDTHS)


def _cparams(sem):
    return pltpu.CompilerParams(dimension_semantics=sem, vmem_limit_bytes=VMEM_LIMIT)


def _const_spec(shape):
    nd = len(shape)
    return pl.BlockSpec(shape, lambda *_: (0,) * nd)


def _mm(a, b):
    return jnp.dot(a.astype(BF16), b.astype(BF16), preferred_element_type=F32)


def _mm_nt(a, b):
    return lax.dot_general(a.astype(BF16), b.astype(BF16), (((1,), (1,)), ((), ())),
                           preferred_element_type=F32)


def _parts(x, n):
    out, r = [], x
    for i in range(n):
        p = r.astype(BF16)
        out.append(p)
        if i + 1 < n:
            r = r - p.astype(F32)
    return out


def _mm_xl(a, b_exact, n):
    acc = None
    for p in _parts(a, n):
        t = jnp.dot(p, b_exact, preferred_element_type=F32)
        acc = t if acc is None else acc + t
    return acc


def _mm_xr(a_exact, b, n):
    acc = None
    for p in _parts(b, n):
        t = jnp.dot(a_exact, p, preferred_element_type=F32)
        acc = t if acc is None else acc + t
    return acc


def _seg_sum(x, bd):
    return _mm_xl(x, bd, 3)


def _sigmoid(x):
    return 1.0 / (1.0 + jnp.exp(-x))


def _silu(x):
    return x * _sigmoid(x)


def _softplus(x):
    return jnp.maximum(x, 0.0) + jnp.log1p(jnp.exp(-jnp.abs(x)))


def _gelu_tanh(x):
    return 0.5 * x * (1.0 + jnp.tanh(0.7978845608028654 * (x + 0.044715 * (x * x * x))))


def _rms(x, gain):
    return x * lax.rsqrt(jnp.mean(x * x, axis=-1, keepdims=True) + RMS_EPS) * gain


def _proj_kernel(x_ref, g_ref, w_ref, wdt_ref, bd_ref, gq_ref, gk_ref,
                 q_out, k_out, v_out, z_out, xbc_out, dt_out, c_out, d_out, *, attn_layout):
    h32 = _rms(x_ref[...], g_ref[...])
    h = h32.astype(BF16)
    bd = bd_ref[...]
    outs = (q_out, k_out, v_out, z_out, xbc_out, dt_out, c_out, d_out)
    lo = 0
    for idx, (width, out) in enumerate(zip(_P_WIDTHS, outs)):
        if out is dt_out:
            h_lo = (h32 - h.astype(F32)).astype(BF16)
            w_hi, w_lo = _parts(wdt_ref[...], 2)
            mm = lambda a, b: jnp.dot(a, b, preferred_element_type=F32)
            y = mm(h, w_hi) + (mm(h, w_lo) + mm(h_lo, w_hi))
        else:
            y = jnp.dot(h, w_ref[:, lo:lo + width], preferred_element_type=F32)
        if idx < 2:
            gain = (gq_ref, gk_ref)[idx][...]
            y = y * lax.rsqrt(_seg_sum(y * y, bd) * (1.0 / HEAD_DIM) + RMS_EPS) * gain
        if attn_layout and out is q_out:
            for hd in range(HEADS):
                q_out[0, hd] = y[:, hd * HEAD_DIM:(hd + 1) * HEAD_DIM]
        elif attn_layout and (out is k_out or out is v_out):
            out[0] = y.T
        else:
            out[...] = y
        lo += width


def _proj(x2, g, w, w_dt, bd, gq, gk, tm, seq, attn_layout):
    n = x2.shape[0]
    row = lambda wd: pl.BlockSpec((tm, wd), lambda i: (i, 0))
    out_specs = [row(wd) for wd in _P_WIDTHS]
    out_shape = [jax.ShapeDtypeStruct((n, wd), F32) for wd in _P_WIDTHS]
    if attn_layout:
        nb, per = n // seq, seq // tm
        out_specs[0] = pl.BlockSpec((1, HEADS, tm, HEAD_DIM), lambda i: (i // per, 0, i % per, 0))
        out_shape[0] = jax.ShapeDtypeStruct((nb, HEADS, seq, HEAD_DIM), F32)
        for j in (1, 2):
            out_specs[j] = pl.BlockSpec((1, GW, tm), lambda i: (i // per, 0, i % per))
            out_shape[j] = jax.ShapeDtypeStruct((nb, GW, seq), F32)
    return pl.pallas_call(
        functools.partial(_proj_kernel, attn_layout=attn_layout),
        grid=(n // tm,),
        in_specs=[row(D_MODEL),
                  _const_spec((1, D_MODEL)), _const_spec((D_MODEL, _P_TOTAL)), _const_spec((D_MODEL, LANE)),
                  _const_spec((GW, GW)), _const_spec((1, GW)), _const_spec((1, GW))],
        out_specs=out_specs,
        out_shape=out_shape,
        compiler_params=_cparams(("parallel",)),
        name="proj",
    )(x2, g, w, w_dt, bd, gq, gk)


def _sb_weights(neg_zs, readables, u, carry, cumsum_pieces):
    state = _sb_logs(neg_zs, readables, u, carry, cumsum_pieces)
    return _sb_finish(readables, *state)


def _sb_logs(neg_zs, readables, u, carry, cumsum_pieces):
    lks, lbs = [], []
    for nz, readable in zip(neg_zs, readables):
        lk = jnp.minimum(nz, 0.0) - jnp.log2(1.0 + jnp.exp2(-jnp.abs(nz)))
        lbs.append(lk - nz)
        lks.append(lk if readable is None else jnp.where(readable, lk, 0.0))
    tails = [_mm_xl(lk, u, cumsum_pieces) for lk in lks]
    scales = []
    for lk in lks:
        scales.append(jnp.exp2(carry))
        carry = carry + jnp.sum(lk, axis=1, keepdims=True)
    return lbs, tails, scales, carry


def _sb_finish(readables, lbs, tails, scales, carry):
    ws = []
    for readable, lb, tail in zip(readables, lbs, tails):
        w = jnp.exp2(lb + tail)
        ws.append(w if readable is None else jnp.where(readable, w, 0.0))
    return ws, scales, carry


def _attn_prompt_kernel(bias_ref, q_ref, k_ref, v_ref, u_ref, gain_ref, o_ref, acc_ref, cs_ref, nz_ref, *, tq, tk):
    h = pl.program_id(1)
    i = pl.program_id(2)
    neg_bias = bias_ref[h]
    u = u_ref[...]
    acc_ref[...] = jnp.zeros_like(acc_ref)
    cs_ref[...] = jnp.zeros_like(cs_ref)
    q = q_ref[0, 0].astype(BF16)
    cg = tq // tk
    key_minus_query = lax.broadcasted_iota(I32, (tq, tk), 1) - lax.broadcasted_iota(I32, (tq, tk), 0)

    def chunk_start(g, d):
        return pl.multiple_of(jnp.maximum(g, 0) * tq + (cg - 1 - d) * tk, tk)

    def logits(g):
        return [jnp.dot(q, k_ref[0, :, pl.ds(chunk_start(g, d), tk)].astype(BF16), preferred_element_type=F32)
                + neg_bias for d in range(cg)]

    def group(g, masked):
        starts = [chunk_start(g, d) for d in range(cg)]
        readables = [(key_minus_query < i * tq - ks) if masked else None for ks in starts]
        state = _sb_logs([nz_ref[d] for d in range(cg)], readables, u, cs_ref[...], 1)
        next_nz = logits(g - 1)
        ws, scales, carry = _sb_finish(readables, *state)
        pvs = [lax.dot_general(w.astype(BF16), v_ref[0, :, pl.ds(ks, tk)].astype(BF16), (((1,), (1,)), ((), ())),
                               preferred_element_type=F32) for w, ks in zip(ws, starts)]
        for d in range(cg):
            nz_ref[d] = next_nz[d]
        acc = acc_ref[...]
        for sc, pv in zip(scales, pvs):
            acc = acc + sc * pv
        acc_ref[...] = acc
        cs_ref[...] = carry

    for d, nz in enumerate(logits(i)):
        nz_ref[d] = nz
    group(i, True)

    def past_body(j, carry):
        group(i - 1 - j, False)
        return carry

    lax.fori_loop(0, i, past_body, 0)
    o_ref[0, 0] = _rms(acc_ref[...], gain_ref[0])


def _attn_prompt(q, k, v, bias, gain, tq, tk):
    b, h, l, d = q.shape
    u = (jnp.arange(tk)[:, None] > jnp.arange(tk)[None, :]).astype(BF16)
    kern = functools.partial(_attn_prompt_kernel, tq=tq, tk=tk)
    return pl.pallas_call(
        kern,
        grid=(b, h, l // tq),
        in_specs=[pl.BlockSpec(memory_space=pltpu.SMEM),
                  pl.BlockSpec((1, 1, tq, d), lambda bi, hi, qi: (bi, hi, qi, 0)),
                  pl.BlockSpec((1, d, l), lambda bi, hi, qi: (bi, hi, 0)),
                  pl.BlockSpec((1, d, l), lambda bi, hi, qi: (bi, hi, 0)),
                  _const_spec((tk, tk)),
                  pl.BlockSpec((1, 1, d), lambda bi, hi, qi: (hi, 0, 0))],
        out_specs=pl.BlockSpec((1, 1, tq, d), lambda bi, hi, qi: (bi, hi, qi, 0)),
        out_shape=jax.ShapeDtypeStruct((b, h, l, d), F32),
        scratch_shapes=[pltpu.VMEM((tq, d), F32), pltpu.VMEM((tq, 1), F32), pltpu.VMEM((tq // tk, tq, tk), F32)],
        compiler_params=_cparams(("parallel", "parallel", "arbitrary")),
        name="attn_prompt",
    )(bias, q, k, v, u, gain)


def _attn_sample_kernel(pt_ref, q_ref, bias_ref, *refs, n_tok, n_group):
    kp_refs, vp_refs = refs[:n_group], refs[n_group:2 * n_group]
    kn_ref, vn_ref, u_ref, hm_ref, gain_ref, o_ref, acc_ref, cs_ref = refs[2 * n_group:]
    p = pl.program_id(1)
    rows = HEADS * n_tok
    q = q_ref[0].astype(BF16)

    def process(ks, vs, readables, acc, carry):
        neg_zs = [_mm(q, k_t) + bias_ref[...] for k_t in ks]
        ws, scales, carry = _sb_weights(neg_zs, readables, u_ref[...], carry, 2)
        for w, sc, v_t in zip(ws, scales, vs):
            acc = acc + sc * _mm_nt(w, v_t)
        return acc, carry

    @pl.when(p == 0)
    def _():
        tok = lax.rem(lax.broadcasted_iota(I32, (rows, PAGE), 0), n_tok)
        key = lax.broadcasted_iota(I32, (rows, PAGE), 1)
        acc, carry = process([kn_ref[0]], [vn_ref[0]], [key < tok],
                             jnp.zeros(acc_ref.shape, F32), jnp.zeros(cs_ref.shape, F32))
        acc_ref[...] = acc
        cs_ref[...] = carry

    order = list(reversed(range(n_group)))
    acc, carry = process([kp_refs[j][...] for j in order], [vp_refs[j][...] for j in order], [None] * n_group,
                         acc_ref[...], cs_ref[...])
    acc_ref[...] = acc
    cs_ref[...] = carry

    @pl.when(p == pl.num_programs(1) - 1)
    def _():
        o = acc_ref[...] * hm_ref[...]
        ss = jnp.sum(o * o, axis=1, keepdims=True) * (1.0 / HEAD_DIM)
        o_ref[0] = o * lax.rsqrt(ss + RMS_EPS) * gain_ref[...]


def _attn_sample(q16, bias16, cache_k, cache_v, layer, page_table, k_new, v_new, hm16, gain):
    s, rows, _ = q16.shape
    n_pages = page_table.shape[1]
    u = (jnp.arange(PAGE)[:, None] > jnp.arange(PAGE)[None, :]).astype(BF16)

    n_group = next(g for g in (8, 4, 2, 1) if n_pages % g == 0)

    def page_spec(j):
        return pl.BlockSpec((None, None, GW, PAGE),
                            lambda si, p, pt: (layer, pt[si, n_pages - (p + 1) * n_group + j], 0, 0))

    page_specs = [page_spec(j) for j in range(n_group)]
    seq_spec = lambda shape: pl.BlockSpec(shape, lambda si, p, pt: (si, 0, 0))
    cst = lambda shape: pl.BlockSpec(shape, lambda si, p, pt: (0,) * len(shape))
    kern = functools.partial(_attn_sample_kernel, n_tok=rows // HEADS, n_group=n_group)
    return pl.pallas_call(
        kern,
        grid_spec=pltpu.PrefetchScalarGridSpec(
            num_scalar_prefetch=1,
            grid=(s, n_pages // n_group),
            in_specs=[seq_spec((1, rows, GW)), cst((rows, 1))] + page_specs + page_specs
            + [seq_spec((1, GW, PAGE)), seq_spec((1, GW, PAGE)),
               cst((PAGE, PAGE)), cst((rows, GW)), cst((1, GW))],
            out_specs=seq_spec((1, rows, GW)),
            scratch_shapes=[pltpu.VMEM((rows, GW), F32), pltpu.VMEM((rows, 1), F32)]),
        out_shape=jax.ShapeDtypeStruct((s, rows, GW), F32),
        compiler_params=_cparams(("parallel", "arbitrary")),
        name="attn_sample",
    )(page_table, q16, bias16, *([cache_k] * n_group), *([cache_v] * n_group), k_new, v_new, u, hm16, gain)


def _ssd_kernel(z_ref, x_ref, dt_ref, c0_ref, s0_ref, cw_ref, cb_ref, dtb_ref, a_ref, dsk_ref, nrm_ref, lt_ref,
                y_ref, st_out, xf_ref, st_ref, *, n_valid):
    t = CHUNK
    c = pl.program_id(1)

    @pl.when(c == 0)
    def _():
        xf_ref[0:SUBLANE, :] = c0_ref[0]
        st_ref[...] = s0_ref[0]

    xf_ref[SUBLANE:SUBLANE + t, :] = x_ref[0]
    conv = cb_ref[...]
    for i in range(SSM_CONV):
        conv = conv + cw_ref[i:i + 1, :] * xf_ref[pl.ds(SUBLANE - (SSM_CONV - 1) + i, t), :]
    xf_ref[0:SUBLANE, :] = xf_ref[t:t + SUBLANE, :]
    xc = _silu(conv)
    xs = xc[:, :GW]
    bm = xc[:, GW:GW + SSM_GROUPS * SSM_STATE]
    cm = xc[:, GW + SSM_GROUPS * SSM_STATE:]

    dt = _softplus(dt_ref[0] + dtb_ref[...])
    if n_valid < t:
        dt = jnp.where(lax.broadcasted_iota(I32, dt.shape, 0) < n_valid, dt, 0.0)
    acum = _mm_xr(lt_ref[...], dt * a_ref[...], 3)
    acum_t = acum.T
    dt_t = dt.T
    a_last = acum[t - 1:t, :]

    row = lax.broadcasted_iota(I32, (t, t), 0)
    col = lax.broadcasted_iota(I32, (t, t), 1)
    causal = row >= col
    lane = lax.broadcasted_iota(I32, (1, GW), 1)
    gmat = [_mm_nt(cm[:, g * SSM_STATE:(g + 1) * SSM_STATE], bm[:, g * SSM_STATE:(g + 1) * SSM_STATE])
            for g in range(SSM_GROUPS)]
    st_prev = st_ref[...]
    y = jnp.zeros((t, GW), F32)
    st_new = jnp.zeros((SSM_STATE, GW), F32)
    e_acum = jnp.zeros((t, GW), F32)
    a_end = jnp.zeros((1, GW), F32)
    for h in range(HEADS):
        g = h // (HEADS // SSM_GROUPS)
        mh = (lane // HEAD_DIM == h).astype(F32)
        col_h = acum[:, h:h + 1]
        decay = jnp.exp(jnp.where(causal, col_h - acum_t[h:h + 1, :], -1e30))
        scores = gmat[g] * decay * dt_t[h:h + 1, :]
        xm = xs * mh
        y = y + _mm(scores, xm)
        to_end = jnp.exp(a_last[:, h:h + 1] - col_h) * dt[:, h:h + 1]
        bw = bm[:, g * SSM_STATE:(g + 1) * SSM_STATE] * to_end
        st_new = st_new + _mm(bw.T, xm)
        e_acum = e_acum + mh * jnp.exp(col_h)
        a_end = a_end + mh * a_last[:, h:h + 1]
    y_in = jnp.where(lane < GW // SSM_GROUPS, _mm(cm[:, :SSM_STATE], st_prev), _mm(cm[:, SSM_STATE:], st_prev))
    st = st_prev * jnp.exp(a_end) + st_new
    st_ref[...] = st
    st_out[0] = st
    y = y + y_in * e_acum + dsk_ref[...] * xs
    y = y * _silu(z_ref[0])
    y_ref[0] = _rms(y, nrm_ref[...])


def _ssd(z, xbc, dt, conv0, st0, cw, cb, dtb, a_neg, dsk, nrm, n_valid):
    b, l, _ = z.shape
    lt = (jnp.arange(CHUNK)[:, None] >= jnp.arange(CHUNK)[None, :]).astype(BF16)
    blk = lambda wd: pl.BlockSpec((1, CHUNK, wd), lambda bi, ci: (bi, ci, 0))
    per_b = lambda shape: pl.BlockSpec(shape, lambda bi, ci: (bi, 0, 0))
    kern = functools.partial(_ssd_kernel, n_valid=n_valid)
    return pl.pallas_call(
        kern,
        grid=(b, l // CHUNK),
        in_specs=[blk(GW), blk(SSM_CONV_DIM), blk(LANE),
                  per_b((1, SUBLANE, SSM_CONV_DIM)), per_b((1, SSM_STATE, GW)),
                  _const_spec((SSM_CONV, SSM_CONV_DIM)), _const_spec((1, SSM_CONV_DIM)),
                  _const_spec((1, LANE)), _const_spec((1, LANE)), _const_spec((1, GW)), _const_spec((1, GW)),
                  _const_spec((CHUNK, CHUNK))],
        out_specs=[blk(GW), per_b((1, SSM_STATE, GW))],
        out_shape=[jax.ShapeDtypeStruct((b, l, GW), F32), jax.ShapeDtypeStruct((b, SSM_STATE, GW), F32)],
        scratch_shapes=[pltpu.VMEM((CHUNK + SUBLANE, SSM_CONV_DIM), F32), pltpu.VMEM((SSM_STATE, GW), F32)],
        compiler_params=_cparams(("parallel", "arbitrary")),
        name="ssd",
    )(z, xbc, dt, conv0, st0, cw, cb, dtb, a_neg, dsk, nrm, lt)


def _rwkv_kernel(p_ref, sh0_ref, s0_ref, mu_ref, w0_ref, a0_ref, dup_ref, iup_ref, gup_ref,
                 kk_ref, ka_ref, rk_ref, gnw_ref, gnb_ref, bd_ref, lt_ref,
                 y_ref, s_out, pf_ref, st_ref, *, bb, t, n_valid):
    c = pl.program_id(1)
    bd = bd_ref[...]
    lane = lax.broadcasted_iota(I32, (1, GW), 1)
    head_mask = [(lane // HEAD_DIM == h).astype(F32) for h in range(HEADS)]
    stack = lambda x: jnp.concatenate([x * m for m in head_mask], axis=0)
    unstack = lambda x: sum(x[h * t:(h + 1) * t] for h in range(HEADS))
    step_r = lax.broadcasted_iota(I32, (HEADS * t, HEADS * t), 0) % t
    step_c = lax.broadcasted_iota(I32, (HEADS * t, HEADS * t), 1) % t
    nt = lambda x, y: lax.dot_general(x, y, (((1,), (1,)), ((), ())), preferred_element_type=F32)
    tn = lambda x, y: lax.dot_general(x, y, (((0,), (0,)), ((), ())), preferred_element_type=F32)
    mm = lambda x, y: jnp.dot(x, y, preferred_element_type=F32)

    @pl.when(c == 0)
    def _():
        for b in range(bb):
            pf_ref[b, 0:SUBLANE, :] = sh0_ref[b]
            st_ref[b] = stack(s0_ref[b])

    seqs = []
    for b in range(bb):
        pf_ref[b, SUBLANE:SUBLANE + t, :] = p_ref[b]
        cur = p_ref[b]
        prev = pf_ref[b, pl.ds(SUBLANE - 1, t), :]
        pf_ref[b, 0:SUBLANE, :] = pf_ref[b, t:t + SUBLANE, :]
        xs = cur + (prev - cur) * mu_ref[...]
        r = xs[:, 0:GW]
        k = xs[:, GW:2 * GW]
        v = xs[:, 2 * GW:3 * GW]
        lo = xs[:, RWKV_LORA_LO:RWKV_PROJ]
        w_log = -_softplus(-(w0_ref[...] + _mm(jnp.tanh(lo), dup_ref[...]))) - 0.5
        a = _sigmoid(a0_ref[...] + _mm(lo, iup_ref[...]))
        kkr = k * kk_ref[...]
        kk = kkr / jnp.maximum(jnp.sqrt(_seg_sum(kkr * kkr, bd)), 1e-12)
        kmod = k * (1.0 + (a - 1.0) * ka_ref[...])
        ka = kk * a
        log_w = -jnp.exp(w_log)
        if n_valid < t:
            valid = lax.broadcasted_iota(I32, (t, GW), 0) < n_valid
            log_w = jnp.where(valid, log_w, 0.0)
            kk, ka, kmod, v = (jnp.where(valid, x, 0.0) for x in (kk, ka, kmod, v))
        bonus = _seg_sum(r * kmod * rk_ref[...], bd) * v
        gate = _mm(_sigmoid(lo), gup_ref[...])

        cum = _mm_xr(lt_ref[...], log_w, 3)
        inv_g = jnp.exp(-cum)
        seqs.append(dict(
            a=stack(kk * jnp.exp(cum - log_w)).astype(BF16), b=stack(ka * inv_g).astype(BF16),
            k=stack(kmod * inv_g).astype(BF16), r=stack(r * jnp.exp(cum)).astype(BF16), v=stack(v).astype(BF16),
            s0=st_ref[b], g_end=jnp.exp(cum[t - 1:t, :]), bonus=bonus, gate=gate))

    strict = step_r > step_c
    incl = step_r >= step_c
    for s in seqs:
        s["s0_b"] = s["s0"].astype(BF16)
        s["power"] = jnp.where(strict, -nt(s["a"], s["b"]), 0.0).astype(BF16)
        s["l_ak"] = jnp.where(strict, nt(s["a"], s["k"]), 0.0).astype(BF16)
    for s in seqs:
        s["u"] = nt(s["a"], s["s0_b"]) + mm(s["l_ak"], s["v"])
    for s in seqs:
        s["l_rk"] = jnp.where(incl, nt(s["r"], s["k"]), 0.0).astype(BF16)
        s["l_rb"] = jnp.where(incl, nt(s["r"], s["b"]), 0.0).astype(BF16)
    for s in seqs:
        s["u"] = s["u"] + mm(s["power"], s["u"].astype(BF16))
    for _ in range(t.bit_length() - 2):
        for s in seqs:
            s["power"] = mm(s["power"], s["power"]).astype(BF16)
        for s in seqs:
            s["u"] = s["u"] + mm(s["power"], s["u"].astype(BF16))
    for s in seqs:
        s["u_b"] = s["u"].astype(BF16)
        s["y"] = unstack(nt(s["r"], s["s0_b"]) + mm(s["l_rk"], s["v"]) - mm(s["l_rb"], s["u_b"]))
    for b, s in enumerate(seqs):
        st_ref[b] = (s["s0"] + tn(s["v"], s["k"]) - tn(s["u_b"], s["b"])) * s["g_end"]

    for b, s in enumerate(seqs):
        y = s["y"]
        mean = _seg_sum(y, bd) * (1.0 / HEAD_DIM)
        yc = y - mean
        var = _seg_sum(yc * yc, bd) * (1.0 / HEAD_DIM)
        yn = yc * lax.rsqrt(var + RWKV_GN_EPS) * gnw_ref[...] + gnb_ref[...]
        y_ref[b] = (yn + s["bonus"]) * s["gate"]
        s_out[b] = unstack(st_ref[b])


def _rwkv(pc, sh0, s0, mu, w0, a0, dup, iup, gup, k_k, k_a, r_k, gnw, gnb, bd, bb, n_valid):
    b, l, _ = pc.shape
    t = HEAD_DIM
    lt = (jnp.arange(t)[:, None] >= jnp.arange(t)[None, :]).astype(BF16)
    blk = lambda wd: pl.BlockSpec((bb, t, wd), lambda bi, ci: (bi, ci, 0))
    per_b = lambda shape: pl.BlockSpec(shape, lambda bi, ci: (bi, 0, 0))
    vec = lambda wd: _const_spec((1, wd))
    kern = functools.partial(_rwkv_kernel, bb=bb, t=t, n_valid=n_valid)
    return pl.pallas_call(
        kern,
        grid=(b // bb, l // t),
        in_specs=[blk(RWKV_PROJ), per_b((bb, SUBLANE, RWKV_PROJ)), per_b((bb, HEAD_DIM, GW)),
                  vec(RWKV_PROJ), vec(GW), vec(GW),
                  _const_spec((RWKV_LORA_W, GW)), _const_spec((RWKV_LORA_W, GW)), _const_spec((RWKV_LORA_W, GW)),
                  vec(GW), vec(GW), vec(GW), vec(GW), vec(GW),
                  _const_spec((GW, GW)), _const_spec((t, t))],
        out_specs=[blk(GW), per_b((bb, HEAD_DIM, GW))],
        out_shape=[jax.ShapeDtypeStruct((b, l, GW), F32), jax.ShapeDtypeStruct((b, HEAD_DIM, GW), F32)],
        scratch_shapes=[pltpu.VMEM((bb, t + SUBLANE, RWKV_PROJ), F32), pltpu.VMEM((bb, HEADS * HEAD_DIM, GW), F32)],
        compiler_params=_cparams(("parallel", "arbitrary")),
        name="rwkv",
    )(pc, sh0, s0, mu, w0, a0, dup, iup, gup, k_k, k_a, r_k, gnw, gnb, bd, lt)


def _sgu_kernel(p_ref, w_ref, bias_ref, gv_ref, bd_ref, o_ref, v_out):
    t = CHUNK
    p = p_ref[0]
    u = _gelu_tanh(p[:, :GW])
    v = _gelu_tanh(p[:, GW:])
    vn = v * lax.rsqrt(_seg_sum(v * v, bd_ref[...]) * (1.0 / HEAD_DIM) + RMS_EPS) * gv_ref[...]
    v_out[0] = vn
    causal = lax.broadcasted_iota(I32, (t, t), 0) >= lax.broadcasted_iota(I32, (t, t), 1)
    lane = lax.broadcasted_iota(I32, (1, GW), 1)
    mixed = bias_ref[...]
    for g in range(HEADS):
        wg = jnp.where(causal, w_ref[g], 0.0)
        mixed = mixed + _mm(wg, vn * (lane // HEAD_DIM == g).astype(F32))
    o_ref[0] = u * mixed


def _sgu(pd, w, bias, gv, bd):
    b, l, _ = pd.shape
    return pl.pallas_call(
        _sgu_kernel,
        grid=(b, l // CHUNK),
        in_specs=[pl.BlockSpec((1, CHUNK, 2 * GW), lambda bi, ci: (bi, ci, 0)),
                  _const_spec((HEADS, CHUNK, CHUNK)), _const_spec((CHUNK, GW)), _const_spec((1, GW)),
                  _const_spec((GW, GW))],
        out_specs=[pl.BlockSpec((1, CHUNK, GW), lambda bi, ci: (bi, ci, 0))] * 2,
        out_shape=[jax.ShapeDtypeStruct((b, l, GW), F32)] * 2,
        compiler_params=_cparams(("parallel", "parallel")),
        name="sgu",
    )(pd, w, bias, gv, bd)


def _outproj_kernel(x_ref, oa_ref, ob_ref, oc_ref, od_ref, w_ref, o_ref, *, oa_heads):
    acc = x_ref[...]
    if oa_heads:
        for hd in range(HEADS):
            acc = acc + jnp.dot(oa_ref[0, hd].astype(BF16), w_ref[hd * HEAD_DIM:(hd + 1) * HEAD_DIM, :],
                                preferred_element_type=F32)
    for i, r in enumerate((oa_ref, ob_ref, oc_ref, od_ref)):
        if i > 0 or not oa_heads:
            acc = acc + jnp.dot(r[...].astype(BF16), w_ref[i * GW:(i + 1) * GW, :], preferred_element_type=F32)
    o_ref[...] = acc


def _outproj(x2, oa, ob, oc, od, w, tm):
    n = x2.shape[0]
    row = lambda wd: pl.BlockSpec((tm, wd), lambda i: (i, 0))
    oa_heads = oa.ndim == 4
    if oa_heads:
        per = oa.shape[2] // tm
        oa_spec = pl.BlockSpec((1, HEADS, tm, HEAD_DIM), lambda i: (i // per, 0, i % per, 0))
    else:
        oa_spec = row(GW)
    return pl.pallas_call(
        functools.partial(_outproj_kernel, oa_heads=oa_heads),
        grid=(n // tm,),
        in_specs=[row(D_MODEL), oa_spec, row(GW), row(GW), row(GW), _const_spec((4 * GW, D_MODEL))],
        out_specs=row(D_MODEL),
        out_shape=jax.ShapeDtypeStruct((n, D_MODEL), F32),
        compiler_params=_cparams(("parallel",)),
        name="outproj",
    )(x2, oa, ob, oc, od, w)


def _ffn_kernel(x_ref, g_ref, wg_ref, wu_ref, wd_ref, o_ref, h_s):
    f = pl.program_id(1)

    @pl.when(f == 0)
    def _():
        x = x_ref[...]
        h_s[...] = _rms(x, g_ref[...]).astype(BF16)
        o_ref[...] = x

    h = h_s[...]
    a = jnp.dot(h, wg_ref[...], preferred_element_type=F32)
    u = jnp.dot(h, wu_ref[...], preferred_element_type=F32)
    o_ref[...] += jnp.dot((_silu(a) * u).astype(BF16), wd_ref[...], preferred_element_type=F32)


def _ffn(x2, g, wg, wu, wd, tm, tf):
    n = x2.shape[0]
    d_ff = wg.shape[1]
    return pl.pallas_call(
        _ffn_kernel,
        grid=(n // tm, d_ff // tf),
        in_specs=[pl.BlockSpec((tm, D_MODEL), lambda i, f: (i, 0)), _const_spec((1, D_MODEL)),
                  pl.BlockSpec((D_MODEL, tf), lambda i, f: (0, f)),
                  pl.BlockSpec((D_MODEL, tf), lambda i, f: (0, f)),
                  pl.BlockSpec((tf, D_MODEL), lambda i, f: (f, 0))],
        out_specs=pl.BlockSpec((tm, D_MODEL), lambda i, f: (i, 0)),
        out_shape=jax.ShapeDtypeStruct((n, D_MODEL), F32),
        scratch_shapes=[pltpu.VMEM((tm, D_MODEL), BF16)],
        compiler_params=_cparams(("parallel", "arbitrary")),
        name="ffn",
    )(x2, g, wg, wu, wd)


def _router_kernel(x_ref, g_ref, rwt_ref, idx_ref, gate_ref):
    h = _rms(x_ref[...], g_ref[...])
    hh, hl = _parts(h, 2)
    wh, wl = _parts(rwt_ref[...], 2)
    nt = lambda a, b: lax.dot_general(a, b, (((1,), (1,)), ((), ())), preferred_element_type=F32)
    logits = nt(wh, hh) + (nt(wh, hl) + nt(wl, hh))
    e_id = lax.broadcasted_iota(I32, logits.shape, 0)
    m1 = jnp.max(logits, axis=0, keepdims=True)
    i1 = jnp.min(jnp.where(logits == m1, e_id, N_EXPERTS), axis=0, keepdims=True)
    rest = jnp.where(e_id == i1, -jnp.inf, logits)
    m2 = jnp.max(rest, axis=0, keepdims=True)
    i2 = jnp.min(jnp.where(rest == m2, e_id, N_EXPERTS), axis=0, keepdims=True)
    e = jnp.exp(m2 - m1)
    g1 = 1.0 / (1.0 + e)
    idx_ref[...] = jnp.where(e_id == 0, i1, jnp.where(e_id == 1, i2, 0))
    gate_ref[...] = jnp.where(e_id == 0, g1, jnp.where(e_id == 1, e * g1, 0.0))


def _router(x2, g, rwt, tm):
    n = x2.shape[0]
    return pl.pallas_call(
        _router_kernel,
        grid=(n // tm,),
        in_specs=[pl.BlockSpec((tm, D_MODEL), lambda i: (i, 0)), _const_spec((1, D_MODEL)),
                  _const_spec((N_EXPERTS, D_MODEL))],
        out_specs=[pl.BlockSpec((N_EXPERTS, tm), lambda i: (0, i))] * 2,
        out_shape=[jax.ShapeDtypeStruct((N_EXPERTS, n), I32), jax.ShapeDtypeStruct((N_EXPERTS, n), F32)],
        compiler_params=_cparams(("parallel",)),
        name="router",
    )(x2, g, rwt)


def _moe_kernel(tok_ref, off_ref, cnt_ref, gts_ref, x_ref, g_ref, wg_ref, wu_ref, wd_ref, o_ref,
                h_s, xg_s, og_s, *, gr):
    b = pl.program_id(0)
    e = pl.program_id(1)
    f = pl.program_id(2)
    n = cnt_ref[b * N_EXPERTS + e]
    o0 = off_ref[b * N_EXPERTS + e]

    @pl.when((e == 0) & (f == 0))
    def _():
        x = x_ref[...]
        h_s[...] = _rms(x, g_ref[...])
        o_ref[...] = x

    @pl.when((b == 0) & (e == 0) & (f == 0))
    def _():
        xg_s[...] = jnp.zeros_like(xg_s)

    def row_loop(body, unroll=4):
        def main(i, carry):
            for r in range(unroll):
                body(i * unroll + r)
            return carry

        def tail(i, carry):
            body(i)
            return carry
        lax.fori_loop(0, n // unroll, main, 0)
        lax.fori_loop((n // unroll) * unroll, n, tail, 0)

    @pl.when(f == 0)
    def _():
        def gather(i):
            xg_s[pl.ds(i, 1), :] = h_s[pl.ds(tok_ref[o0 + i], 1), :]
        row_loop(gather)

    def group(gi, carry):
        r0 = pl.multiple_of(gi * gr, gr)
        xb = xg_s[pl.ds(r0, gr), :].astype(BF16)
        a = jnp.dot(xb, wg_ref[...], preferred_element_type=F32)
        u = jnp.dot(xb, wu_ref[...], preferred_element_type=F32)
        y = jnp.dot((_silu(a) * u).astype(BF16), wd_ref[...], preferred_element_type=F32)

        @pl.when(f == 0)
        def _():
            og_s[pl.ds(r0, gr), :] = y

        @pl.when(f > 0)
        def _():
            og_s[pl.ds(r0, gr), :] += y
        return carry

    lax.fori_loop(0, (n + gr - 1) // gr, group, 0)

    @pl.when(f == pl.num_programs(2) - 1)
    def _():
        def scatter(i):
            tk = tok_ref[o0 + i]
            o_ref[pl.ds(tk, 1), :] += gts_ref[o0 + i] * og_s[pl.ds(i, 1), :]
        row_loop(scatter)


def _moe(x2, g, tok, off, cnt, gts, wg, wu, wd, tm, tf, gr):
    n = x2.shape[0]
    d_ff = wg.shape[2]
    kern = functools.partial(_moe_kernel, gr=gr)
    return pl.pallas_call(
        kern,
        grid_spec=pltpu.PrefetchScalarGridSpec(
            num_scalar_prefetch=3,
            grid=(n // tm, N_EXPERTS, d_ff // tf),
            in_specs=[pl.BlockSpec(memory_space=pltpu.SMEM),
                      pl.BlockSpec((tm, D_MODEL), lambda b, e, f, *_: (b, 0)),
                      pl.BlockSpec((1, D_MODEL), lambda b, e, f, *_: (0, 0)),
                      pl.BlockSpec((None, D_MODEL, tf), lambda b, e, f, *_: (e, 0, f)),
                      pl.BlockSpec((None, D_MODEL, tf), lambda b, e, f, *_: (e, 0, f)),
                      pl.BlockSpec((None, tf, D_MODEL), lambda b, e, f, *_: (e, f, 0))],
            out_specs=pl.BlockSpec((tm, D_MODEL), lambda b, e, f, *_: (b, 0)),
            scratch_shapes=[pltpu.VMEM((tm, D_MODEL), F32), pltpu.VMEM((-(-tm // gr) * gr, D_MODEL), F32),
                            pltpu.VMEM((-(-tm // gr) * gr, D_MODEL), F32)]),
        out_shape=jax.ShapeDtypeStruct((n, D_MODEL), F32),
        compiler_params=_cparams(("arbitrary", "arbitrary", "arbitrary")),
        name="moe",
    )(tok, off, cnt, gts, x2, g, wg, wu, wd)


def _moe_group_rows(tm):
    return min(tm, -(-(5 * tm // 16) // 64) * 64)


def _moe_plan(idx, gate, tm):
    n = idx.shape[1]
    nb = n // tm
    e = idx[:2].T.reshape(nb, 2 * tm)
    gt = gate[:2].T.reshape(nb, 2 * tm)
    order = jnp.argsort(e, axis=1, stable=True)
    tok = (order // 2).astype(I32)
    gts = jnp.take_along_axis(gt, order, axis=1)
    cnt = jnp.sum(e[:, :, None] == jnp.arange(N_EXPERTS, dtype=I32)[None, None, :], axis=1).astype(I32)
    off = jnp.cumsum(cnt, axis=1) - cnt + (jnp.arange(nb, dtype=I32) * (2 * tm))[:, None]
    return tok.reshape(-1), off.reshape(-1).astype(I32), cnt.reshape(-1), gts.reshape(-1)


def _layer(x, l, p, consts, past, cache_k, cache_v, page_table):
    b, seq, _ = x.shape
    n = b * seq
    is_sample = past is not None
    tm = min(n, 256)
    x2 = x.reshape(n, D_MODEL)
    bd, hm = consts["bd"], consts["hm"]

    qn, kn, v, z, xbc, dtr, pc, pd = _proj(x2, p["norm_mix"], p["w_in"], p["w_dt"], bd, p["gq"], p["gk"], tm,
                                           seq, not is_sample)

    if not is_sample:
        from_feature_major = lambda a: a.reshape(b, HEADS, HEAD_DIM, seq).transpose(0, 3, 1, 2)
        k_new, v_new = from_feature_major(kn), from_feature_major(v)
        oa = _attn_prompt(qn, kn, v, p["sb_bias"], p["sb_out_norm"].reshape(HEADS, 1, HEAD_DIM),
                          min(seq, ATTN_TQ), min(seq, ATTN_TK))
    else:
        k_new = kn.reshape(b, seq, HEADS, HEAD_DIM)
        v_new = v.reshape(b, seq, HEADS, HEAD_DIM)
        rows = HEADS * seq
        q16 = (qn.reshape(b, 1, seq, GW) * hm[None, :, None, :]).reshape(b, rows, GW)
        pad_keys_t = lambda a: jnp.pad(a.reshape(b, seq, GW), ((0, 0), (0, PAGE - seq), (0, 0))).transpose(0, 2, 1)
        o16 = _attn_sample(q16, jnp.repeat(p["sb_bias"], seq).reshape(rows, 1), cache_k, cache_v, l, page_table,
                           pad_keys_t(kn), pad_keys_t(v), jnp.repeat(hm, seq, axis=0),
                           p["sb_out_norm"].reshape(1, GW))
        oa = o16.reshape(b, HEADS, seq, GW).sum(axis=1).reshape(n, GW)

    lc = -(-seq // CHUNK) * CHUNK
    pad_seq = lambda a, to: jnp.pad(a.reshape(b, seq, -1), ((0, 0), (0, to - seq), (0, 0)))

    xbc3 = xbc.reshape(b, seq, SSM_CONV_DIM)
    if is_sample:
        conv0 = past["conv"]
        st0 = past["ssm"].reshape(b, GW, SSM_STATE).transpose(0, 2, 1)
    else:
        conv0 = jnp.zeros((b, SSM_CONV - 1, SSM_CONV_DIM), F32)
        st0 = jnp.zeros((b, SSM_STATE, GW), F32)
    conv8 = jnp.pad(conv0, ((0, 0), (SUBLANE - (SSM_CONV - 1), 0), (0, 0)))
    ob, st = _ssd(pad_seq(z, lc), pad_seq(xbc, lc), pad_seq(dtr, lc), conv8, st0,
                  p["conv_w"], p["conv_b"], p["dt_bias"], p["a_neg"], p["d_skip"], p["ssm_norm"], min(seq, CHUNK))
    ob = ob[:, :seq].reshape(n, GW)
    ssm_new = st.transpose(0, 2, 1).reshape(b, HEADS, HEAD_DIM, SSM_STATE)
    conv_new = jnp.concatenate([conv0, xbc3], axis=1)[:, seq:]

    pc3 = pc.reshape(b, seq, RWKV_PROJ)
    if is_sample:
        shift0 = past["shift"]
        wkv0 = past["wkv"].transpose(0, 2, 1, 3).reshape(b, HEAD_DIM, GW)
    else:
        shift0 = jnp.zeros((b, RWKV_PROJ), F32)
        wkv0 = jnp.zeros((b, HEAD_DIM, GW), F32)
    sh8 = jnp.pad(shift0[:, None, :], ((0, 0), (SUBLANE - 1, 0), (0, 0)))
    lr = -(-seq // HEAD_DIM) * HEAD_DIM
    oc, wkv = _rwkv(pad_seq(pc, lr), sh8, wkv0, p["shift_mu"], p["decay_w0"], p["iclr_a0"],
                    p["decay_up"], p["iclr_up"], p["gate_up"], p["k_k"], p["k_a"], p["r_k"], p["gn_w"], p["gn_b"],
                    bd, 2, min(seq, HEAD_DIM))
    oc = oc[:, :seq].reshape(n, GW)
    wkv_new = wkv.reshape(b, HEAD_DIM, HEADS, HEAD_DIM).transpose(0, 2, 1, 3)
    shift_new = pc3[:, -1]

    od, sgu_v = _sgu(pad_seq(pd, lc), p["sgu_w"], p["sgu_bias"], p["sgu_v_norm"], bd)
    od = od[:, :seq].reshape(n, GW)
    sgu_v = sgu_v[:, :seq].reshape(b, seq, HEADS, HEAD_DIM)

    x_mid = _outproj(x2, oa, ob, oc, od, p["w_out"], tm)

    if "ffn_wg" in p:
        tmf = min(n, 512)
        x_out = _ffn(x_mid, p["norm_ffn"], p["ffn_wg"], p["ffn_wu"], p["ffn_wd"], tmf, p["ffn_wg"].shape[1] // 2)
    else:
        tmm = min(n, 1024)
        idx, gate = _router(x_mid, p["norm_ffn"], p["router_wt"], min(n, 512))
        tok, off, cnt, gts = _moe_plan(idx, gate, tmm)
        x_out = _moe(x_mid, p["norm_ffn"], tok, off, cnt, gts, p["moe_wg"], p["moe_wu"], p["moe_wd"],
                     tmm, 896, _moe_group_rows(tmm))
    return x_out.reshape(b, seq, D_MODEL), (k_new, v_new, ssm_new, conv_new, wkv_new, shift_new, sgu_v)


def _trunk(x, layers, consts, pasts, cache_k, cache_v, page_table):
    states = []
    for l, p in enumerate(layers):
        x, st = _layer(x, l, p, consts, None if pasts is None else pasts[l], cache_k, cache_v, page_table)
        states.append(st)
    return x, [jnp.stack(s) for s in zip(*states)]


def _pad_rows(w, lo, total):
    return jnp.pad(w, ((lo, total - lo - w.shape[0]), (0, 0)))


def kernel(x_prompt, x_sample, cache_k, cache_v, page_table, state_ssm, state_conv, state_wkv, state_shift, norm_mix, norm_ffn, w_in, w_out, q_norm, k_norm, sb_bias, sb_out_norm, conv_w, conv_b, dt_bias, a_log, d_skip, ssm_norm, shift_mu, decay_w0, decay_up, iclr_a0, iclr_up, gate_up, k_k, k_a, r_k, gn_w, gn_b, sgu_w, sgu_b, sgu_v_norm, ffn_wg, ffn_wu, ffn_wd, router_w, moe_wg, moe_wu, moe_wd):
    depth = w_in.shape[0]
    lane = jnp.arange(GW)
    consts = {
        "bd": (lane[:, None] // HEAD_DIM == lane[None, :] // HEAD_DIM).astype(BF16),
        "hm": (jnp.arange(HEADS)[:, None] == lane[None, :] // HEAD_DIM).astype(F32),
    }
    dt_col = 4 * GW + SSM_CONV_DIM
    row = lambda a: a.reshape(1, -1).astype(F32)
    layers = []
    for l in range(depth):
        w = w_in[l]
        w_packed = jnp.concatenate(
            [w[:, :dt_col + HEADS], jnp.zeros((D_MODEL, LANE - HEADS), F32), w[:, dt_col + HEADS:]], axis=1)
        p = dict(
            norm_mix=row(norm_mix[l]), norm_ffn=row(norm_ffn[l]),
            w_in=w_packed.astype(BF16), w_dt=w_packed[:, dt_col:dt_col + LANE], w_out=w_out[l].astype(BF16),
            gq=row(jnp.tile(q_norm[l], HEADS)) * -(LOG2E * HEAD_DIM ** -0.5), gk=row(jnp.tile(k_norm[l], HEADS)),
            sb_bias=sb_bias[l].astype(F32) * -LOG2E, sb_out_norm=sb_out_norm[l],
            conv_w=conv_w[l], conv_b=row(conv_b[l]),
            dt_bias=jnp.pad(row(dt_bias[l]), ((0, 0), (0, LANE - HEADS))),
            a_neg=jnp.pad(row(-jnp.exp(a_log[l])), ((0, 0), (0, LANE - HEADS))),
            d_skip=row(jnp.repeat(d_skip[l], HEAD_DIM)), ssm_norm=row(ssm_norm[l]),
            shift_mu=row(shift_mu[l]), decay_w0=row(decay_w0[l]), iclr_a0=row(iclr_a0[l]),
            decay_up=_pad_rows(decay_up[l], 0, RWKV_LORA_W).astype(BF16),
            iclr_up=_pad_rows(iclr_up[l], DECAY_LORA, RWKV_LORA_W).astype(BF16),
            gate_up=_pad_rows(gate_up[l], DECAY_LORA + ICLR_LORA, RWKV_LORA_W).astype(BF16),
            k_k=row(k_k[l]), k_a=row(k_a[l]), r_k=row(r_k[l]), gn_w=row(gn_w[l]), gn_b=row(gn_b[l]),
            sgu_w=sgu_w[l][:, :CHUNK, :CHUNK],
            sgu_bias=jnp.repeat(sgu_b[l][:, :CHUNK].T, HEAD_DIM, axis=1),
            sgu_v_norm=row(sgu_v_norm[l]),
        )
        if l % 2 == 0:
            p.update(ffn_wg=ffn_wg[l // 2].astype(BF16), ffn_wu=ffn_wu[l // 2].astype(BF16),
                     ffn_wd=ffn_wd[l // 2].astype(BF16))
        else:
            p.update(router_wt=router_w[l // 2].T, moe_wg=moe_wg[l // 2].astype(BF16),
                     moe_wu=moe_wu[l // 2].astype(BF16), moe_wd=moe_wd[l // 2].astype(BF16))
        layers.append(p)

    to_pages_t = lambda c: c.transpose(0, 1, 3, 4, 2).reshape(c.shape[0], c.shape[1], GW, PAGE)
    ck, cv = to_pages_t(cache_k), to_pages_t(cache_v)
    pasts = [dict(ssm=state_ssm[l], conv=state_conv[l], wkv=state_wkv[l], shift=state_shift[l])
             for l in range(depth)]

    y_p, (k_p, v_p, ssm_p, conv_p, wkv_p, shift_p, _) = _trunk(x_prompt, layers, consts, None, ck, cv, page_table)
    y_s, (k_s, v_s, ssm_s, conv_s, wkv_s, shift_s, sgu_v_s) = _trunk(x_sample, layers, consts, pasts, ck, cv, page_table)
    return (y_p, y_s, k_p, v_p, k_s, v_s, ssm_p, ssm_s, conv_p, conv_s, wkv_p, wkv_s, shift_p, shift_s, sgu_v_s)
```

```python
import functools

import jax
import jax.numpy as jnp
from jax import lax
from jax.experimental import pallas as pl
from jax.experimental.pallas import tpu as pltpu

F32, BF16, I32 = jnp.float32, jnp.bfloat16, jnp.int32

D_MODEL = 1024
HEAD_DIM = 64
HEADS = 4
GW = HEADS * HEAD_DIM
RMS_EPS = 1e-6
LOG2E = 1.4426950408889634
PAGE = 128
SSM_STATE = 128
SSM_GROUPS = 2
SSM_CONV = 4
SSM_CONV_DIM = GW + 2 * SSM_GROUPS * SSM_STATE
CHUNK = 128
RWKV_PROJ = 896
RWKV_LORA_LO = 3 * GW
RWKV_LORA_W = RWKV_PROJ - RWKV_LORA_LO
DECAY_LORA, ICLR_LORA, GATE_LORA = 32, 32, 64
RWKV_GN_EPS = HEAD_DIM * 1e-5
N_EXPERTS = 8
MOE_FF_BLOCKS = 4
LANE = 128
SUBLANE = 8
VMEM_LIMIT = 52 * 1024 * 1024
ATTN_TQ, ATTN_TK = 512, 256

_P_WIDTHS = (GW, GW, GW, GW, SSM_CONV_DIM, LANE, RWKV_PROJ, 2 * GW)
_P_TOTAL = sum(_P_WIDTHS)


def _cparams(sem):
    return pltpu.CompilerParams(dimension_semantics=sem, vmem_limit_bytes=VMEM_LIMIT)


def _const_spec(shape):
    nd = len(shape)
    return pl.BlockSpec(shape, lambda *_: (0,) * nd)


def _mm(a, b):
    return jnp.dot(a.astype(BF16), b.astype(BF16), preferred_element_type=F32)


def _mm_nt(a, b):
    return lax.dot_general(a.astype(BF16), b.astype(BF16), (((1,), (1,)), ((), ())),
                           preferred_element_type=F32)


def _parts(x, n):
    out, r = [], x
    for i in range(n):
        p = r.astype(BF16)
        out.append(p)
        if i + 1 < n:
            r = r - p.astype(F32)
    return out


def _mm_xl(a, b_exact, n):
    acc = None
    for p in _parts(a, n):
        t = jnp.dot(p, b_exact, preferred_element_type=F32)
        acc = t if acc is None else acc + t
    return acc


def _mm_xr(a_exact, b, n):
    acc = None
    for p in _parts(b, n):
        t = jnp.dot(a_exact, p, preferred_element_type=F32)
        acc = t if acc is None else acc + t
    return acc


def _seg_sum(x, bd):
    return _mm_xl(x, bd, 3)


def _sigmoid(x):
    return 1.0 / (1.0 + jnp.exp(-x))


def _silu(x):
    return x * _sigmoid(x)


def _softplus(x):
    return jnp.maximum(x, 0.0) + jnp.log1p(jnp.exp(-jnp.abs(x)))


def _gelu_tanh(x):
    return 0.5 * x * (1.0 + jnp.tanh(0.7978845608028654 * (x + 0.044715 * (x * x * x))))


def _rms(x, gain):
    return x * lax.rsqrt(jnp.mean(x * x, axis=-1, keepdims=True) + RMS_EPS) * gain


def _proj_kernel(x_ref, g_ref, w_ref, wdt_ref, bd_ref, gq_ref, gk_ref,
                 q_out, k_out, v_out, z_out, xbc_out, dt_out, c_out, d_out, *, attn_layout):
    h32 = _rms(x_ref[...], g_ref[...])
    h = h32.astype(BF16)
    bd = bd_ref[...]
    outs = (q_out, k_out, v_out, z_out, xbc_out, dt_out, c_out, d_out)
    lo = 0
    for idx, (width, out) in enumerate(zip(_P_WIDTHS, outs)):
        if out is dt_out:
            h_lo = (h32 - h.astype(F32)).astype(BF16)
            w_hi, w_lo = _parts(wdt_ref[...], 2)
            mm = lambda a, b: jnp.dot(a, b, preferred_element_type=F32)
            y = mm(h, w_hi) + (mm(h, w_lo) + mm(h_lo, w_hi))
        else:
            y = jnp.dot(h, w_ref[:, lo:lo + width], preferred_element_type=F32)
        if idx < 2:
            gain = (gq_ref, gk_ref)[idx][...]
            y = y * lax.rsqrt(_seg_sum(y * y, bd) * (1.0 / HEAD_DIM) + RMS_EPS) * gain
        if attn_layout and out is q_out:
            for hd in range(HEADS):
                q_out[0, hd] = y[:, hd * HEAD_DIM:(hd + 1) * HEAD_DIM]
        elif attn_layout and (out is k_out or out is v_out):
            out[0] = y.T
        else:
            out[...] = y
        lo += width


def _proj(x2, g, w, w_dt, bd, gq, gk, tm, seq, attn_layout):
    n = x2.shape[0]
    row = lambda wd: pl.BlockSpec((tm, wd), lambda i: (i, 0))
    out_specs = [row(wd) for wd in _P_WIDTHS]
    out_shape = [jax.ShapeDtypeStruct((n, wd), F32) for wd in _P_WIDTHS]
    if attn_layout:
        nb, per = n // seq, seq // tm
        out_specs[0] = pl.BlockSpec((1, HEADS, tm, HEAD_DIM), lambda i: (i // per, 0, i % per, 0))
        out_shape[0] = jax.ShapeDtypeStruct((nb, HEADS, seq, HEAD_DIM), F32)
        for j in (1, 2):
            out_specs[j] = pl.BlockSpec((1, GW, tm), lambda i: (i // per, 0, i % per))
            out_shape[j] = jax.ShapeDtypeStruct((nb, GW, seq), F32)
    return pl.pallas_call(
        functools.partial(_proj_kernel, attn_layout=attn_layout),
        grid=(n // tm,),
        in_specs=[row(D_MODEL),
                  _const_spec((1, D_MODEL)), _const_spec((D_MODEL, _P_TOTAL)), _const_spec((D_MODEL, LANE)),
                  _const_spec((GW, GW)), _const_spec((1, GW)), _const_spec((1, GW))],
        out_specs=out_specs,
        out_shape=out_shape,
        compiler_params=_cparams(("parallel",)),
        name="proj",
    )(x2, g, w, w_dt, bd, gq, gk)


def _sb_weights(neg_zs, readables, u, carry, cumsum_pieces):
    state = _sb_logs(neg_zs, readables, u, carry, cumsum_pieces)
    return _sb_finish(readables, *state)


def _sb_logs(neg_zs, readables, u, carry, cumsum_pieces):
    lks, lbs = [], []
    for nz, readable in zip(neg_zs, readables):
        lk = jnp.minimum(nz, 0.0) - jnp.log2(1.0 + jnp.exp2(-jnp.abs(nz)))
        lbs.append(lk - nz)
        lks.append(lk if readable is None else jnp.where(readable, lk, 0.0))
    tails = [_mm_xl(lk, u, cumsum_pieces) for lk in lks]
    scales = []
    for lk in lks:
        scales.append(jnp.exp2(carry))
        carry = carry + jnp.sum(lk, axis=1, keepdims=True)
    return lbs, tails, scales, carry


def _sb_finish(readables, lbs, tails, scales, carry):
    ws = []
    for readable, lb, tail in zip(readables, lbs, tails):
        w = jnp.exp2(lb + tail)
        ws.append(w if readable is None else jnp.where(readable, w, 0.0))
    return ws, scales, carry


def _attn_prompt_kernel(bias_ref, q_ref, k_ref, v_ref, u_ref, gain_ref, o_ref, acc_ref, cs_ref, nz_ref, *, tq, tk):
    h = pl.program_id(1)
    i = pl.program_id(2)
    neg_bias = bias_ref[h]
    u = u_ref[...]
    acc_ref[...] = jnp.zeros_like(acc_ref)
    cs_ref[...] = jnp.zeros_like(cs_ref)
    q = q_ref[0, 0].astype(BF16)
    cg = tq // tk
    key_minus_query = lax.broadcasted_iota(I32, (tq, tk), 1) - lax.broadcasted_iota(I32, (tq, tk), 0)

    def chunk_start(g, d):
        return pl.multiple_of(jnp.maximum(g, 0) * tq + (cg - 1 - d) * tk, tk)

    def logits(g):
        return [jnp.dot(q, k_ref[0, :, pl.ds(chunk_start(g, d), tk)].astype(BF16), preferred_element_type=F32)
                + neg_bias for d in range(cg)]

    def group(g, masked, nzs, acc, carry):
        starts = [chunk_start(g, d) for d in range(cg)]
        readables = [(key_minus_query < i * tq - ks) if masked else None for ks in starts]
        state = _sb_logs(nzs, readables, u, carry, 1)
        next_nzs = logits(g - 1)
        ws, scales, carry = _sb_finish(readables, *state)
        pvs = [lax.dot_general(w.astype(BF16), v_ref[0, :, pl.ds(ks, tk)].astype(BF16), (((1,), (1,)), ((), ())),
                               preferred_element_type=F32) for w, ks in zip(ws, starts)]
        for sc, pv in zip(scales, pvs):
            acc = acc + sc * pv
        return next_nzs, acc, carry

    def load_state():
        return [nz_ref[d] for d in range(cg)], acc_ref[...], cs_ref[...]

    def store_state(nzs, acc, carry):
        for d in range(cg):
            nz_ref[d] = nzs[d]
        acc_ref[...] = acc
        cs_ref[...] = carry

    store_state(*group(i, True, logits(i), jnp.zeros(acc_ref.shape, F32), jnp.zeros(cs_ref.shape, F32)))

    def past_pair(j, carry):
        g = i - 1 - 2 * j
        store_state(*group(g - 1, False, *group(g, False, *load_state())))
        return carry

    lax.fori_loop(0, i // 2, past_pair, 0)

    @pl.when(i % 2 == 1)
    def _():
        store_state(*group(0, False, *load_state()))

    o_ref[0, 0] = _rms(acc_ref[...], gain_ref[0])


def _attn_prompt(q, k, v, bias, gain, tq, tk):
    b, h, l, d = q.shape
    u = (jnp.arange(tk)[:, None] > jnp.arange(tk)[None, :]).astype(BF16)
    kern = functools.partial(_attn_prompt_kernel, tq=tq, tk=tk)
    return pl.pallas_call(
        kern,
        grid=(b, h, l // tq),
        in_specs=[pl.BlockSpec(memory_space=pltpu.SMEM),
                  pl.BlockSpec((1, 1, tq, d), lambda bi, hi, qi: (bi, hi, qi, 0)),
                  pl.BlockSpec((1, d, l), lambda bi, hi, qi: (bi, hi, 0)),
                  pl.BlockSpec((1, d, l), lambda bi, hi, qi: (bi, hi, 0)),
                  _const_spec((tk, tk)),
                  pl.BlockSpec((1, 1, d), lambda bi, hi, qi: (hi, 0, 0))],
        out_specs=pl.BlockSpec((1, 1, tq, d), lambda bi, hi, qi: (bi, hi, qi, 0)),
        out_shape=jax.ShapeDtypeStruct((b, h, l, d), F32),
        scratch_shapes=[pltpu.VMEM((tq, d), F32), pltpu.VMEM((tq, 1), F32), pltpu.VMEM((tq // tk, tq, tk), F32)],
        compiler_params=_cparams(("parallel", "parallel", "arbitrary")),
        name="attn_prompt",
    )(bias, q, k, v, u, gain)


def _attn_sample_kernel(pt_ref, q_ref, bias_ref, *refs, n_tok, n_group):
    kp_refs, vp_refs = refs[:n_group], refs[n_group:2 * n_group]
    kn_ref, vn_ref, u_ref, hm_ref, gain_ref, o_ref, acc_ref, cs_ref = refs[2 * n_group:]
    p = pl.program_id(1)
    rows = HEADS * n_tok
    q = q_ref[0].astype(BF16)

    def process(ks, vs, readables, acc, carry):
        neg_zs = [_mm(q, k_t) + bias_ref[...] for k_t in ks]
        ws, scales, carry = _sb_weights(neg_zs, readables, u_ref[...], carry, 2)
        for w, sc, v_t in zip(ws, scales, vs):
            acc = acc + sc * _mm_nt(w, v_t)
        return acc, carry

    @pl.when(p == 0)
    def _():
        tok = lax.rem(lax.broadcasted_iota(I32, (rows, PAGE), 0), n_tok)
        key = lax.broadcasted_iota(I32, (rows, PAGE), 1)
        acc, carry = process([kn_ref[0]], [vn_ref[0]], [key < tok],
                             jnp.zeros(acc_ref.shape, F32), jnp.zeros(cs_ref.shape, F32))
        acc_ref[...] = acc
        cs_ref[...] = carry

    order = list(reversed(range(n_group)))
    acc, carry = process([kp_refs[j][...] for j in order], [vp_refs[j][...] for j in order], [None] * n_group,
                         acc_ref[...], cs_ref[...])
    acc_ref[...] = acc
    cs_ref[...] = carry

    @pl.when(p == pl.num_programs(1) - 1)
    def _():
        o = acc_ref[...] * hm_ref[...]
        ss = jnp.sum(o * o, axis=1, keepdims=True) * (1.0 / HEAD_DIM)
        o_ref[0] = o * lax.rsqrt(ss + RMS_EPS) * gain_ref[...]


def _attn_sample(q16, bias16, cache_k, cache_v, layer, page_table, k_new, v_new, hm16, gain):
    s, rows, _ = q16.shape
    n_pages = page_table.shape[1]
    u = (jnp.arange(PAGE)[:, None] > jnp.arange(PAGE)[None, :]).astype(BF16)

    n_group = next(g for g in (16, 8, 4, 2, 1) if n_pages % g == 0)

    def page_spec(j):
        return pl.BlockSpec((None, None, GW, PAGE),
                            lambda si, p, pt: (layer, pt[si, n_pages - (p + 1) * n_group + j], 0, 0))

    page_specs = [page_spec(j) for j in range(n_group)]
    seq_spec = lambda shape: pl.BlockSpec(shape, lambda si, p, pt: (si, 0, 0))
    cst = lambda shape: pl.BlockSpec(shape, lambda si, p, pt: (0,) * len(shape))
    kern = functools.partial(_attn_sample_kernel, n_tok=rows // HEADS, n_group=n_group)
    return pl.pallas_call(
        kern,
        grid_spec=pltpu.PrefetchScalarGridSpec(
            num_scalar_prefetch=1,
            grid=(s, n_pages // n_group),
            in_specs=[seq_spec((1, rows, GW)), cst((rows, 1))] + page_specs + page_specs
            + [seq_spec((1, GW, PAGE)), seq_spec((1, GW, PAGE)),
               cst((PAGE, PAGE)), cst((rows, GW)), cst((1, GW))],
            out_specs=seq_spec((1, rows, GW)),
            scratch_shapes=[pltpu.VMEM((rows, GW), F32), pltpu.VMEM((rows, 1), F32)]),
        out_shape=jax.ShapeDtypeStruct((s, rows, GW), F32),
        compiler_params=_cparams(("parallel", "arbitrary")),
        name="attn_sample",
    )(page_table, q16, bias16, *([cache_k] * n_group), *([cache_v] * n_group), k_new, v_new, u, hm16, gain)


def _ssd_kernel(z_ref, x_ref, dt_ref, c0_ref, s0_ref, cw_ref, cb_ref, dtb_ref, a_ref, dsk_ref, nrm_ref, lt_ref,
                y_ref, st_out, xf_ref, st_ref, *, n_valid):
    t = CHUNK
    c = pl.program_id(1)

    @pl.when(c == 0)
    def _():
        xf_ref[0:SUBLANE, :] = c0_ref[0]
        st_ref[...] = s0_ref[0]

    xf_ref[SUBLANE:SUBLANE + t, :] = x_ref[0]
    conv = cb_ref[...]
    for i in range(SSM_CONV):
        conv = conv + cw_ref[i:i + 1, :] * xf_ref[pl.ds(SUBLANE - (SSM_CONV - 1) + i, t), :]
    xf_ref[0:SUBLANE, :] = xf_ref[t:t + SUBLANE, :]
    xc = _silu(conv)
    xs = xc[:, :GW]
    bm = xc[:, GW:GW + SSM_GROUPS * SSM_STATE]
    cm = xc[:, GW + SSM_GROUPS * SSM_STATE:]

    dt = _softplus(dt_ref[0] + dtb_ref[...])
    if n_valid < t:
        dt = jnp.where(lax.broadcasted_iota(I32, dt.shape, 0) < n_valid, dt, 0.0)
    acum = _mm_xr(lt_ref[...], dt * a_ref[...], 3)
    acum_t = acum.T
    dt_t = dt.T
    a_last = acum[t - 1:t, :]

    row = lax.broadcasted_iota(I32, (t, t), 0)
    col = lax.broadcasted_iota(I32, (t, t), 1)
    causal = row >= col
    lane = lax.broadcasted_iota(I32, (1, GW), 1)
    gmat = [_mm_nt(cm[:, g * SSM_STATE:(g + 1) * SSM_STATE], bm[:, g * SSM_STATE:(g + 1) * SSM_STATE])
            for g in range(SSM_GROUPS)]
    st_prev = st_ref[...]
    y = jnp.zeros((t, GW), F32)
    st_new = jnp.zeros((SSM_STATE, GW), F32)
    e_acum = jnp.zeros((t, GW), F32)
    a_end = jnp.zeros((1, GW), F32)
    for h in range(HEADS):
        g = h // (HEADS // SSM_GROUPS)
        mh = (lane // HEAD_DIM == h).astype(F32)
        col_h = acum[:, h:h + 1]
        decay = jnp.exp(jnp.where(causal, col_h - acum_t[h:h + 1, :], -1e30))
        scores = gmat[g] * decay * dt_t[h:h + 1, :]
        xm = xs * mh
        y = y + _mm(scores, xm)
        to_end = jnp.exp(a_last[:, h:h + 1] - col_h) * dt[:, h:h + 1]
        bw = bm[:, g * SSM_STATE:(g + 1) * SSM_STATE] * to_end
        st_new = st_new + _mm(bw.T, xm)
        e_acum = e_acum + mh * jnp.exp(col_h)
        a_end = a_end + mh * a_last[:, h:h + 1]
    y_in = jnp.where(lane < GW // SSM_GROUPS, _mm(cm[:, :SSM_STATE], st_prev), _mm(cm[:, SSM_STATE:], st_prev))
    st = st_prev * jnp.exp(a_end) + st_new
    st_ref[...] = st
    st_out[0] = st
    y = y + y_in * e_acum + dsk_ref[...] * xs
    y = y * _silu(z_ref[0])
    y_ref[0] = _rms(y, nrm_ref[...])


def _ssd(z, xbc, dt, conv0, st0, cw, cb, dtb, a_neg, dsk, nrm, n_valid):
    b, l, _ = z.shape
    lt = (jnp.arange(CHUNK)[:, None] >= jnp.arange(CHUNK)[None, :]).astype(BF16)
    blk = lambda wd: pl.BlockSpec((1, CHUNK, wd), lambda bi, ci: (bi, ci, 0))
    per_b = lambda shape: pl.BlockSpec(shape, lambda bi, ci: (bi, 0, 0))
    kern = functools.partial(_ssd_kernel, n_valid=n_valid)
    return pl.pallas_call(
        kern,
        grid=(b, l // CHUNK),
        in_specs=[blk(GW), blk(SSM_CONV_DIM), blk(LANE),
                  per_b((1, SUBLANE, SSM_CONV_DIM)), per_b((1, SSM_STATE, GW)),
                  _const_spec((SSM_CONV, SSM_CONV_DIM)), _const_spec((1, SSM_CONV_DIM)),
                  _const_spec((1, LANE)), _const_spec((1, LANE)), _const_spec((1, GW)), _const_spec((1, GW)),
                  _const_spec((CHUNK, CHUNK))],
        out_specs=[blk(GW), per_b((1, SSM_STATE, GW))],
        out_shape=[jax.ShapeDtypeStruct((b, l, GW), F32), jax.ShapeDtypeStruct((b, SSM_STATE, GW), F32)],
        scratch_shapes=[pltpu.VMEM((CHUNK + SUBLANE, SSM_CONV_DIM), F32), pltpu.VMEM((SSM_STATE, GW), F32)],
        compiler_params=_cparams(("parallel", "arbitrary")),
        name="ssd",
    )(z, xbc, dt, conv0, st0, cw, cb, dtb, a_neg, dsk, nrm, lt)


def _rwkv_kernel(p_ref, sh0_ref, s0_ref, mu_ref, w0_ref, a0_ref, dup_ref, iup_ref, gup_ref,
                 kk_ref, ka_ref, rk_ref, gnw_ref, gnb_ref, bd_ref, lt_ref,
                 y_ref, s_out, pf_ref, st_ref, *, bb, t, n_valid):
    c = pl.program_id(1)
    bd = bd_ref[...]
    lane = lax.broadcasted_iota(I32, (1, GW), 1)
    head_mask = [(lane // HEAD_DIM == h).astype(F32) for h in range(HEADS)]
    stack = lambda x: jnp.concatenate([x * m for m in head_mask], axis=0)
    unstack = lambda x: sum(x[h * t:(h + 1) * t] for h in range(HEADS))
    step_r = lax.broadcasted_iota(I32, (HEADS * t, HEADS * t), 0) % t
    step_c = lax.broadcasted_iota(I32, (HEADS * t, HEADS * t), 1) % t
    nt = lambda x, y: lax.dot_general(x, y, (((1,), (1,)), ((), ())), preferred_element_type=F32)
    tn = lambda x, y: lax.dot_general(x, y, (((0,), (0,)), ((), ())), preferred_element_type=F32)
    mm = lambda x, y: jnp.dot(x, y, preferred_element_type=F32)

    @pl.when(c == 0)
    def _():
        for b in range(bb):
            pf_ref[b, 0:SUBLANE, :] = sh0_ref[b]
            st_ref[b] = stack(s0_ref[b])

    seqs = []
    for b in range(bb):
        pf_ref[b, SUBLANE:SUBLANE + t, :] = p_ref[b]
        cur = p_ref[b]
        prev = pf_ref[b, pl.ds(SUBLANE - 1, t), :]
        pf_ref[b, 0:SUBLANE, :] = pf_ref[b, t:t + SUBLANE, :]
        xs = cur + (prev - cur) * mu_ref[...]
        r = xs[:, 0:GW]
        k = xs[:, GW:2 * GW]
        v = xs[:, 2 * GW:3 * GW]
        lo = xs[:, RWKV_LORA_LO:RWKV_PROJ]
        w_log = -_softplus(-(w0_ref[...] + _mm(jnp.tanh(lo), dup_ref[...]))) - 0.5
        a = _sigmoid(a0_ref[...] + _mm(lo, iup_ref[...]))
        kkr = k * kk_ref[...]
        kk = kkr / jnp.maximum(jnp.sqrt(_seg_sum(kkr * kkr, bd)), 1e-12)
        kmod = k * (1.0 + (a - 1.0) * ka_ref[...])
        ka = kk * a
        log_w = -jnp.exp(w_log)
        if n_valid < t:
            valid = lax.broadcasted_iota(I32, (t, GW), 0) < n_valid
            log_w = jnp.where(valid, log_w, 0.0)
            kk, ka, kmod, v = (jnp.where(valid, x, 0.0) for x in (kk, ka, kmod, v))
        bonus = _seg_sum(r * kmod * rk_ref[...], bd) * v
        gate = _mm(_sigmoid(lo), gup_ref[...])

        cum = _mm_xr(lt_ref[...], log_w, 3)
        inv_g = jnp.exp(-cum)
        seqs.append(dict(
            a=stack(kk * jnp.exp(cum - log_w)).astype(BF16), b=stack(ka * inv_g).astype(BF16),
            k=stack(kmod * inv_g).astype(BF16), r=stack(r * jnp.exp(cum)).astype(BF16), v=stack(v).astype(BF16),
            s0=st_ref[b], g_end=jnp.exp(cum[t - 1:t, :]), bonus=bonus, gate=gate))

    strict = step_r > step_c
    incl = step_r >= step_c
    for s in seqs:
        s["s0_b"] = s["s0"].astype(BF16)
        s["power"] = jnp.where(strict, -nt(s["a"], s["b"]), 0.0).astype(BF16)
        s["l_ak"] = jnp.where(strict, nt(s["a"], s["k"]), 0.0).astype(BF16)
    for s in seqs:
        s["u"] = nt(s["a"], s["s0_b"]) + mm(s["l_ak"], s["v"])
    for s in seqs:
        s["l_rk"] = jnp.where(incl, nt(s["r"], s["k"]), 0.0).astype(BF16)
        s["l_rb"] = jnp.where(incl, nt(s["r"], s["b"]), 0.0).astype(BF16)
    for s in seqs:
        s["u"] = s["u"] + mm(s["power"], s["u"].astype(BF16))
    for _ in range(t.bit_length() - 2):
        for s in seqs:
            s["power"] = mm(s["power"], s["power"]).astype(BF16)
        for s in seqs:
            s["u"] = s["u"] + mm(s["power"], s["u"].astype(BF16))
    for s in seqs:
        s["u_b"] = s["u"].astype(BF16)
        s["y"] = unstack(nt(s["r"], s["s0_b"]) + mm(s["l_rk"], s["v"]) - mm(s["l_rb"], s["u_b"]))
    for b, s in enumerate(seqs):
        st_ref[b] = (s["s0"] + tn(s["v"], s["k"]) - tn(s["u_b"], s["b"])) * s["g_end"]

    for b, s in enumerate(seqs):
        y = s["y"]
        mean = _seg_sum(y, bd) * (1.0 / HEAD_DIM)
        yc = y - mean
        var = _seg_sum(yc * yc, bd) * (1.0 / HEAD_DIM)
        yn = yc * lax.rsqrt(var + RWKV_GN_EPS) * gnw_ref[...] + gnb_ref[...]
        y_ref[b] = (yn + s["bonus"]) * s["gate"]
        s_out[b] = unstack(st_ref[b])


def _rwkv(pc, sh0, s0, mu, w0, a0, dup, iup, gup, k_k, k_a, r_k, gnw, gnb, bd, bb, n_valid):
    b, l, _ = pc.shape
    t = HEAD_DIM
    lt = (jnp.arange(t)[:, None] >= jnp.arange(t)[None, :]).astype(BF16)
    blk = lambda wd: pl.BlockSpec((bb, t, wd), lambda bi, ci: (bi, ci, 0))
    per_b = lambda shape: pl.BlockSpec(shape, lambda bi, ci: (bi, 0, 0))
    vec = lambda wd: _const_spec((1, wd))
    kern = functools.partial(_rwkv_kernel, bb=bb, t=t, n_valid=n_valid)
    return pl.pallas_call(
        kern,
        grid=(b // bb, l // t),
        in_specs=[blk(RWKV_PROJ), per_b((bb, SUBLANE, RWKV_PROJ)), per_b((bb, HEAD_DIM, GW)),
                  vec(RWKV_PROJ), vec(GW), vec(GW),
                  _const_spec((RWKV_LORA_W, GW)), _const_spec((RWKV_LORA_W, GW)), _const_spec((RWKV_LORA_W, GW)),
                  vec(GW), vec(GW), vec(GW), vec(GW), vec(GW),
                  _const_spec((GW, GW)), _const_spec((t, t))],
        out_specs=[blk(GW), per_b((bb, HEAD_DIM, GW))],
        out_shape=[jax.ShapeDtypeStruct((b, l, GW), F32), jax.ShapeDtypeStruct((b, HEAD_DIM, GW), F32)],
        scratch_shapes=[pltpu.VMEM((bb, t + SUBLANE, RWKV_PROJ), F32), pltpu.VMEM((bb, HEADS * HEAD_DIM, GW), F32)],
        compiler_params=_cparams(("parallel", "arbitrary")),
        name="rwkv",
    )(pc, sh0, s0, mu, w0, a0, dup, iup, gup, k_k, k_a, r_k, gnw, gnb, bd, lt)


def _sgu_kernel(p_ref, w_ref, bias_ref, gv_ref, bd_ref, o_ref, v_out, *, n_sub):
    t = CHUNK
    p = p_ref[0]
    u = _gelu_tanh(p[:, :GW])
    v = _gelu_tanh(p[:, GW:])
    vn = v * lax.rsqrt(_seg_sum(v * v, bd_ref[...]) * (1.0 / HEAD_DIM) + RMS_EPS) * gv_ref[...]
    v_out[0] = vn
    causal = lax.broadcasted_iota(I32, (t, t), 0) >= lax.broadcasted_iota(I32, (t, t), 1)
    lane = lax.broadcasted_iota(I32, (1, GW), 1)
    w_tril = [jnp.where(causal, w_ref[g], 0.0).astype(BF16) for g in range(HEADS)]
    head = [(lane // HEAD_DIM == g).astype(F32) for g in range(HEADS)]
    for s in range(n_sub):
        rows = slice(s * t, (s + 1) * t)
        mixed = bias_ref[...]
        for g in range(HEADS):
            mixed = mixed + _mm(w_tril[g], vn[rows] * head[g])
        o_ref[0, rows, :] = u[rows] * mixed


def _sgu(pd, w, bias, gv, bd):
    b, l, _ = pd.shape
    n_sub = next(s for s in (4, 2, 1) if l % (s * CHUNK) == 0)
    rows = n_sub * CHUNK
    return pl.pallas_call(
        functools.partial(_sgu_kernel, n_sub=n_sub),
        grid=(b, l // rows),
        in_specs=[pl.BlockSpec((1, rows, 2 * GW), lambda bi, ci: (bi, ci, 0)),
                  _const_spec((HEADS, CHUNK, CHUNK)), _const_spec((CHUNK, GW)), _const_spec((1, GW)),
                  _const_spec((GW, GW))],
        out_specs=[pl.BlockSpec((1, rows, GW), lambda bi, ci: (bi, ci, 0))] * 2,
        out_shape=[jax.ShapeDtypeStruct((b, l, GW), F32)] * 2,
        compiler_params=_cparams(("parallel", "parallel")),
        name="sgu",
    )(pd, w, bias, gv, bd)


def _outproj_kernel(x_ref, oa_ref, ob_ref, oc_ref, od_ref, w_ref, o_ref, *, oa_heads):
    acc = x_ref[...]
    if oa_heads:
        for hd in range(HEADS):
            acc = acc + jnp.dot(oa_ref[0, hd].astype(BF16), w_ref[hd * HEAD_DIM:(hd + 1) * HEAD_DIM, :],
                                preferred_element_type=F32)
    for i, r in enumerate((oa_ref, ob_ref, oc_ref, od_ref)):
        if i > 0 or not oa_heads:
            acc = acc + jnp.dot(r[...].astype(BF16), w_ref[i * GW:(i + 1) * GW, :], preferred_element_type=F32)
    o_ref[...] = acc


def _outproj(x2, oa, ob, oc, od, w, tm):
    n = x2.shape[0]
    row = lambda wd: pl.BlockSpec((tm, wd), lambda i: (i, 0))
    oa_heads = oa.ndim == 4
    if oa_heads:
        per = oa.shape[2] // tm
        oa_spec = pl.BlockSpec((1, HEADS, tm, HEAD_DIM), lambda i: (i // per, 0, i % per, 0))
    else:
        oa_spec = row(GW)
    return pl.pallas_call(
        functools.partial(_outproj_kernel, oa_heads=oa_heads),
        grid=(n // tm,),
        in_specs=[row(D_MODEL), oa_spec, row(GW), row(GW), row(GW), _const_spec((4 * GW, D_MODEL))],
        out_specs=row(D_MODEL),
        out_shape=jax.ShapeDtypeStruct((n, D_MODEL), F32),
        compiler_params=_cparams(("parallel",)),
        name="outproj",
    )(x2, oa, ob, oc, od, w)


def _ffn_kernel(x_ref, g_ref, wg_ref, wu_ref, wd_ref, o_ref, h_s):
    f = pl.program_id(1)

    @pl.when(f == 0)
    def _():
        x = x_ref[...]
        h_s[...] = _rms(x, g_ref[...]).astype(BF16)
        o_ref[...] = x

    h = h_s[...]
    a = jnp.dot(h, wg_ref[...], preferred_element_type=F32)
    u = jnp.dot(h, wu_ref[...], preferred_element_type=F32)
    o_ref[...] += jnp.dot((_silu(a) * u).astype(BF16), wd_ref[...], preferred_element_type=F32)


def _ffn(x2, g, wg, wu, wd, tm, tf):
    n = x2.shape[0]
    d_ff = wg.shape[1]
    return pl.pallas_call(
        _ffn_kernel,
        grid=(n // tm, d_ff // tf),
        in_specs=[pl.BlockSpec((tm, D_MODEL), lambda i, f: (i, 0)), _const_spec((1, D_MODEL)),
                  pl.BlockSpec((D_MODEL, tf), lambda i, f: (0, f)),
                  pl.BlockSpec((D_MODEL, tf), lambda i, f: (0, f)),
                  pl.BlockSpec((tf, D_MODEL), lambda i, f: (f, 0))],
        out_specs=pl.BlockSpec((tm, D_MODEL), lambda i, f: (i, 0)),
        out_shape=jax.ShapeDtypeStruct((n, D_MODEL), F32),
        scratch_shapes=[pltpu.VMEM((tm, D_MODEL), BF16)],
        compiler_params=_cparams(("parallel", "arbitrary")),
        name="ffn",
    )(x2, g, wg, wu, wd)


def _router_kernel(x_ref, g_ref, rwt_ref, idx_ref, gate_ref):
    h = _rms(x_ref[...], g_ref[...])
    hh, hl = _parts(h, 2)
    wh, wl = _parts(rwt_ref[...], 2)
    nt = lambda a, b: lax.dot_general(a, b, (((1,), (1,)), ((), ())), preferred_element_type=F32)
    logits = nt(wh, hh) + (nt(wh, hl) + nt(wl, hh))
    e_id = lax.broadcasted_iota(I32, logits.shape, 0)
    m1 = jnp.max(logits, axis=0, keepdims=True)
    i1 = jnp.min(jnp.where(logits == m1, e_id, N_EXPERTS), axis=0, keepdims=True)
    rest = jnp.where(e_id == i1, -jnp.inf, logits)
    m2 = jnp.max(rest, axis=0, keepdims=True)
    i2 = jnp.min(jnp.where(rest == m2, e_id, N_EXPERTS), axis=0, keepdims=True)
    e = jnp.exp(m2 - m1)
    g1 = 1.0 / (1.0 + e)
    idx_ref[...] = jnp.where(e_id == 0, i1, jnp.where(e_id == 1, i2, 0))
    gate_ref[...] = jnp.where(e_id == 0, g1, jnp.where(e_id == 1, e * g1, 0.0))


def _router(x2, g, rwt, tm):
    n = x2.shape[0]
    return pl.pallas_call(
        _router_kernel,
        grid=(n // tm,),
        in_specs=[pl.BlockSpec((tm, D_MODEL), lambda i: (i, 0)), _const_spec((1, D_MODEL)),
                  _const_spec((N_EXPERTS, D_MODEL))],
        out_specs=[pl.BlockSpec((N_EXPERTS, tm), lambda i: (0, i))] * 2,
        out_shape=[jax.ShapeDtypeStruct((N_EXPERTS, n), I32), jax.ShapeDtypeStruct((N_EXPERTS, n), F32)],
        compiler_params=_cparams(("parallel",)),
        name="router",
    )(x2, g, rwt)


def _moe_kernel(tok_ref, off_ref, cnt_ref, gts_ref, x_ref, g_ref, wg_ref, wu_ref, wd_ref, o_ref,
                h_s, xg_s, og_s, *, gr):
    b = pl.program_id(0)
    e = pl.program_id(1)
    f = pl.program_id(2)
    n = cnt_ref[b * N_EXPERTS + e]
    o0 = off_ref[b * N_EXPERTS + e]

    @pl.when((e == 0) & (f == 0))
    def _():
        x = x_ref[...]
        h_s[...] = _rms(x, g_ref[...])
        o_ref[...] = x

    @pl.when((b == 0) & (e == 0) & (f == 0))
    def _():
        xg_s[...] = jnp.zeros_like(xg_s)

    def row_loop(body, unroll=4):
        def main(i, carry):
            for r in range(unroll):
                body(i * unroll + r)
            return carry

        def tail(i, carry):
            body(i)
            return carry
        lax.fori_loop(0, n // unroll, main, 0)
        lax.fori_loop((n // unroll) * unroll, n, tail, 0)

    @pl.when(f == 0)
    def _():
        def gather(i):
            xg_s[pl.ds(i, 1), :] = h_s[pl.ds(tok_ref[o0 + i], 1), :]
        row_loop(gather)

    def group(gi, carry):
        r0 = pl.multiple_of(gi * gr, gr)
        xb = xg_s[pl.ds(r0, gr), :].astype(BF16)
        a = jnp.dot(xb, wg_ref[...], preferred_element_type=F32)
        u = jnp.dot(xb, wu_ref[...], preferred_element_type=F32)
        y = jnp.dot((_silu(a) * u).astype(BF16), wd_ref[...], preferred_element_type=F32)

        @pl.when(f == 0)
        def _():
            og_s[pl.ds(r0, gr), :] = y

        @pl.when(f > 0)
        def _():
            og_s[pl.ds(r0, gr), :] += y
        return carry

    lax.fori_loop(0, (n + gr - 1) // gr, group, 0)

    @pl.when(f == pl.num_programs(2) - 1)
    def _():
        def scatter(i):
            tk = tok_ref[o0 + i]
            o_ref[pl.ds(tk, 1), :] += gts_ref[o0 + i] * og_s[pl.ds(i, 1), :]
        row_loop(scatter)


def _moe(x2, g, tok, off, cnt, gts, wg, wu, wd, tm, gr):
    n = x2.shape[0]
    n_f, tf = wg.shape[1], wg.shape[3]
    kern = functools.partial(_moe_kernel, gr=gr)
    return pl.pallas_call(
        kern,
        grid_spec=pltpu.PrefetchScalarGridSpec(
            num_scalar_prefetch=3,
            grid=(n // tm, N_EXPERTS, n_f),
            in_specs=[pl.BlockSpec(memory_space=pltpu.SMEM),
                      pl.BlockSpec((tm, D_MODEL), lambda b, e, f, *_: (b, 0)),
                      pl.BlockSpec((1, D_MODEL), lambda b, e, f, *_: (0, 0)),
                      pl.BlockSpec((None, None, D_MODEL, tf), lambda b, e, f, *_: (e, f, 0, 0)),
                      pl.BlockSpec((None, None, D_MODEL, tf), lambda b, e, f, *_: (e, f, 0, 0)),
                      pl.BlockSpec((None, tf, D_MODEL), lambda b, e, f, *_: (e, f, 0))],
            out_specs=pl.BlockSpec((tm, D_MODEL), lambda b, e, f, *_: (b, 0)),
            scratch_shapes=[pltpu.VMEM((tm, D_MODEL), F32), pltpu.VMEM((-(-tm // gr) * gr, D_MODEL), F32),
                            pltpu.VMEM((-(-tm // gr) * gr, D_MODEL), F32)]),
        out_shape=jax.ShapeDtypeStruct((n, D_MODEL), F32),
        compiler_params=_cparams(("arbitrary", "arbitrary", "arbitrary")),
        name="moe",
    )(tok, off, cnt, gts, x2, g, wg, wu, wd)


def _moe_group_rows(tm):
    return min(tm, -(-(9 * tm // 32) // 32) * 32)


def _moe_plan(idx, gate, tm):
    n = idx.shape[1]
    nb = n // tm
    e = idx[:2].T.reshape(nb, 2 * tm)
    gt = gate[:2].T.reshape(nb, 2 * tm)
    order = jnp.argsort(e, axis=1, stable=True)
    tok = (order // 2).astype(I32)
    gts = jnp.take_along_axis(gt, order, axis=1)
    cnt = jnp.sum(e[:, :, None] == jnp.arange(N_EXPERTS, dtype=I32)[None, None, :], axis=1).astype(I32)
    off = jnp.cumsum(cnt, axis=1) - cnt + (jnp.arange(nb, dtype=I32) * (2 * tm))[:, None]
    return tok.reshape(-1), off.reshape(-1).astype(I32), cnt.reshape(-1), gts.reshape(-1)


def _layer(x, l, p, consts, past, cache_k, cache_v, page_table):
    b, seq, _ = x.shape
    n = b * seq
    is_sample = past is not None
    tm = min(n, 512)
    x2 = x.reshape(n, D_MODEL)
    bd, hm = consts["bd"], consts["hm"]

    qn, kn, v, z, xbc, dtr, pc, pd = _proj(x2, p["norm_mix"], p["w_in"], p["w_dt"], bd, p["gq"], p["gk"], tm,
                                           seq, not is_sample)

    if not is_sample:
        from_feature_major = lambda a: a.reshape(b, HEADS, HEAD_DIM, seq).transpose(0, 3, 1, 2)
        k_new, v_new = from_feature_major(kn), from_feature_major(v)
        oa = _attn_prompt(qn, kn, v, p["sb_bias"], p["sb_out_norm"].reshape(HEADS, 1, HEAD_DIM),
                          min(seq, ATTN_TQ), min(seq, ATTN_TK))
    else:
        k_new = kn.reshape(b, seq, HEADS, HEAD_DIM)
        v_new = v.reshape(b, seq, HEADS, HEAD_DIM)
        rows = HEADS * seq
        q16 = (qn.reshape(b, 1, seq, GW) * hm[None, :, None, :]).reshape(b, rows, GW)
        pad_keys_t = lambda a: jnp.pad(a.reshape(b, seq, GW), ((0, 0), (0, PAGE - seq), (0, 0))).transpose(0, 2, 1)
        o16 = _attn_sample(q16, jnp.repeat(p["sb_bias"], seq).reshape(rows, 1), cache_k, cache_v, l, page_table,
                           pad_keys_t(kn), pad_keys_t(v), jnp.repeat(hm, seq, axis=0),
                           p["sb_out_norm"].reshape(1, GW))
        oa = o16.reshape(b, HEADS, seq, GW).sum(axis=1).reshape(n, GW)

    lc = -(-seq // CHUNK) * CHUNK
    pad_seq = lambda a, to: jnp.pad(a.reshape(b, seq, -1), ((0, 0), (0, to - seq), (0, 0)))

    xbc3 = xbc.reshape(b, seq, SSM_CONV_DIM)
    if is_sample:
        conv0 = past["conv"]
        st0 = past["ssm"].reshape(b, GW, SSM_STATE).transpose(0, 2, 1)
    else:
        conv0 = jnp.zeros((b, SSM_CONV - 1, SSM_CONV_DIM), F32)
        st0 = jnp.zeros((b, SSM_STATE, GW), F32)
    conv8 = jnp.pad(conv0, ((0, 0), (SUBLANE - (SSM_CONV - 1), 0), (0, 0)))
    ob, st = _ssd(pad_seq(z, lc), pad_seq(xbc, lc), pad_seq(dtr, lc), conv8, st0,
                  p["conv_w"], p["conv_b"], p["dt_bias"], p["a_neg"], p["d_skip"], p["ssm_norm"], min(seq, CHUNK))
    ob = ob[:, :seq].reshape(n, GW)
    ssm_new = st.transpose(0, 2, 1).reshape(b, HEADS, HEAD_DIM, SSM_STATE)
    conv_new = jnp.concatenate([conv0, xbc3], axis=1)[:, seq:]

    pc3 = pc.reshape(b, seq, RWKV_PROJ)
    if is_sample:
        shift0 = past["shift"]
        wkv0 = past["wkv"].transpose(0, 2, 1, 3).reshape(b, HEAD_DIM, GW)
    else:
        shift0 = jnp.zeros((b, RWKV_PROJ), F32)
        wkv0 = jnp.zeros((b, HEAD_DIM, GW), F32)
    sh8 = jnp.pad(shift0[:, None, :], ((0, 0), (SUBLANE - 1, 0), (0, 0)))
    lr = -(-seq // HEAD_DIM) * HEAD_DIM
    oc, wkv = _rwkv(pad_seq(pc, lr), sh8, wkv0, p["shift_mu"], p["decay_w0"], p["iclr_a0"],
                    p["decay_up"], p["iclr_up"], p["gate_up"], p["k_k"], p["k_a"], p["r_k"], p["gn_w"], p["gn_b"],
                    bd, 2, min(seq, HEAD_DIM))
    oc = oc[:, :seq].reshape(n, GW)
    wkv_new = wkv.reshape(b, HEAD_DIM, HEADS, HEAD_DIM).transpose(0, 2, 1, 3)
    shift_new = pc3[:, -1]

    od, sgu_v = _sgu(pad_seq(pd, lc), p["sgu_w"], p["sgu_bias"], p["sgu_v_norm"], bd)
    od = od[:, :seq].reshape(n, GW)
    sgu_v = sgu_v[:, :seq].reshape(b, seq, HEADS, HEAD_DIM)

    x_mid = _outproj(x2, oa, ob, oc, od, p["w_out"], tm)

    if "ffn_wg" in p:
        tmf = min(n, 512)
        x_out = _ffn(x_mid, p["norm_ffn"], p["ffn_wg"], p["ffn_wu"], p["ffn_wd"], tmf, p["ffn_wg"].shape[1] // 2)
    else:
        tmm = min(n, 1024)
        idx, gate = _router(x_mid, p["norm_ffn"], p["router_wt"], min(n, 512))
        tok, off, cnt, gts = _moe_plan(idx, gate, tmm)
        x_out = _moe(x_mid, p["norm_ffn"], tok, off, cnt, gts, p["moe_wg"], p["moe_wu"], p["moe_wd"],
                     tmm, _moe_group_rows(tmm))
    return x_out.reshape(b, seq, D_MODEL), (k_new, v_new, ssm_new, conv_new, wkv_new, shift_new, sgu_v)


def _trunk(x, layers, consts, pasts, cache_k, cache_v, page_table):
    states = []
    for l, p in enumerate(layers):
        x, st = _layer(x, l, p, consts, None if pasts is None else pasts[l], cache_k, cache_v, page_table)
        states.append(st)
    return x, [jnp.stack(s) for s in zip(*states)]


def _pad_rows(w, lo, total):
    return jnp.pad(w, ((lo, total - lo - w.shape[0]), (0, 0)))


def kernel(x_prompt, x_sample, cache_k, cache_v, page_table, state_ssm, state_conv, state_wkv, state_shift, norm_mix, norm_ffn, w_in, w_out, q_norm, k_norm, sb_bias, sb_out_norm, conv_w, conv_b, dt_bias, a_log, d_skip, ssm_norm, shift_mu, decay_w0, decay_up, iclr_a0, iclr_up, gate_up, k_k, k_a, r_k, gn_w, gn_b, sgu_w, sgu_b, sgu_v_norm, ffn_wg, ffn_wu, ffn_wd, router_w, moe_wg, moe_wu, moe_wd):
    depth = w_in.shape[0]
    lane = jnp.arange(GW)
    consts = {
        "bd": (lane[:, None] // HEAD_DIM == lane[None, :] // HEAD_DIM).astype(BF16),
        "hm": (jnp.arange(HEADS)[:, None] == lane[None, :] // HEAD_DIM).astype(F32),
    }
    dt_col = 4 * GW + SSM_CONV_DIM
    row = lambda a: a.reshape(1, -1).astype(F32)
    layers = []
    for l in range(depth):
        w = w_in[l]
        w_packed = jnp.concatenate(
            [w[:, :dt_col + HEADS], jnp.zeros((D_MODEL, LANE - HEADS), F32), w[:, dt_col + HEADS:]], axis=1)
        p = dict(
            norm_mix=row(norm_mix[l]), norm_ffn=row(norm_ffn[l]),
            w_in=w_packed.astype(BF16), w_dt=w_packed[:, dt_col:dt_col + LANE], w_out=w_out[l].astype(BF16),
            gq=row(jnp.tile(q_norm[l], HEADS)) * -(LOG2E * HEAD_DIM ** -0.5), gk=row(jnp.tile(k_norm[l], HEADS)),
            sb_bias=sb_bias[l].astype(F32) * -LOG2E, sb_out_norm=sb_out_norm[l],
            conv_w=conv_w[l], conv_b=row(conv_b[l]),
            dt_bias=jnp.pad(row(dt_bias[l]), ((0, 0), (0, LANE - HEADS))),
            a_neg=jnp.pad(row(-jnp.exp(a_log[l])), ((0, 0), (0, LANE - HEADS))),
            d_skip=row(jnp.repeat(d_skip[l], HEAD_DIM)), ssm_norm=row(ssm_norm[l]),
            shift_mu=row(shift_mu[l]), decay_w0=row(decay_w0[l]), iclr_a0=row(iclr_a0[l]),
            decay_up=_pad_rows(decay_up[l], 0, RWKV_LORA_W).astype(BF16),
            iclr_up=_pad_rows(iclr_up[l], DECAY_LORA, RWKV_LORA_W).astype(BF16),
            gate_up=_pad_rows(gate_up[l], DECAY_LORA + ICLR_LORA, RWKV_LORA_W).astype(BF16),
            k_k=row(k_k[l]), k_a=row(k_a[l]), r_k=row(r_k[l]), gn_w=row(gn_w[l]), gn_b=row(gn_b[l]),
            sgu_w=sgu_w[l][:, :CHUNK, :CHUNK],
            sgu_bias=jnp.repeat(sgu_b[l][:, :CHUNK].T, HEAD_DIM, axis=1),
            sgu_v_norm=row(sgu_v_norm[l]),
        )
        if l % 2 == 0:
            p.update(ffn_wg=ffn_wg[l // 2].astype(BF16), ffn_wu=ffn_wu[l // 2].astype(BF16),
                     ffn_wd=ffn_wd[l // 2].astype(BF16))
        else:
            blocked = lambda w: w.astype(BF16).reshape(N_EXPERTS, D_MODEL, MOE_FF_BLOCKS, -1).transpose(0, 2, 1, 3)
            p.update(router_wt=router_w[l // 2].T, moe_wg=blocked(moe_wg[l // 2]),
                     moe_wu=blocked(moe_wu[l // 2]), moe_wd=moe_wd[l // 2].astype(BF16))
        layers.append(p)

    to_pages_t = lambda c: c.transpose(0, 1, 3, 4, 2).reshape(c.shape[0], c.shape[1], GW, PAGE)
    ck, cv = to_pages_t(cache_k), to_pages_t(cache_v)
    pasts = [dict(ssm=state_ssm[l], conv=state_conv[l], wkv=state_wkv[l], shift=state_shift[l])
             for l in range(depth)]

    y_p, (k_p, v_p, ssm_p, conv_p, wkv_p, shift_p, _) = _trunk(x_prompt, layers, consts, None, ck, cv, page_table)
    y_s, (k_s, v_s, ssm_s, conv_s, wkv_s, shift_s, sgu_v_s) = _trunk(x_sample, layers, consts, pasts, ck, cv, page_table)
    return (y_p, y_s, k_p, v_p, k_s, v_s, ssm_p, ssm_s, conv_p, conv_s, wkv_p, wkv_s, shift_p, shift_s, sgu_v_s)
```

```python
import functools

import jax
import jax.numpy as jnp
from jax import lax
from jax.experimental import pallas as pl
from jax.experimental.pallas import tpu as pltpu

F32, BF16, I32 = jnp.float32, jnp.bfloat16, jnp.int32

D_MODEL = 1024
HEAD_DIM = 64
HEADS = 4
GW = HEADS * HEAD_DIM
RMS_EPS = 1e-6
LOG2E = 1.4426950408889634
PAGE = 128
SSM_STATE = 128
SSM_GROUPS = 2
SSM_CONV = 4
SSM_CONV_DIM = GW + 2 * SSM_GROUPS * SSM_STATE
CHUNK = 128
RWKV_PROJ = 896
RWKV_LORA_LO = 3 * GW
RWKV_LORA_W = RWKV_PROJ - RWKV_LORA_LO
DECAY_LORA, ICLR_LORA, GATE_LORA = 32, 32, 64
RWKV_GN_EPS = HEAD_DIM * 1e-5
N_EXPERTS = 8
MOE_FF_BLOCKS = 2
LANE = 128
SUBLANE = 8
VMEM_LIMIT = 52 * 1024 * 1024
ATTN_TQ, ATTN_TK = 512, 256

_P_WIDTHS = (GW, GW, GW, GW, SSM_CONV_DIM, LANE, RWKV_PROJ, 2 * GW)
_P_TOTAL = sum(_P_WIDTHS)


def _cparams(sem):
    return pltpu.CompilerParams(dimension_semantics=sem, vmem_limit_bytes=VMEM_LIMIT)


def _const_spec(shape):
    nd = len(shape)
    return pl.BlockSpec(shape, lambda *_: (0,) * nd)


def _mm(a, b):
    return jnp.dot(a.astype(BF16), b.astype(BF16), preferred_element_type=F32)


def _mm_nt(a, b):
    return lax.dot_general(a.astype(BF16), b.astype(BF16), (((1,), (1,)), ((), ())),
                           preferred_element_type=F32)


def _parts(x, n):
    out, r = [], x
    for i in range(n):
        p = r.astype(BF16)
        out.append(p)
        if i + 1 < n:
            r = r - p.astype(F32)
    return out


def _mm_xl(a, b_exact, n):
    acc = None
    for p in _parts(a, n):
        t = jnp.dot(p, b_exact, preferred_element_type=F32)
        acc = t if acc is None else acc + t
    return acc


def _mm_xr(a_exact, b, n):
    acc = None
    for p in _parts(b, n):
        t = jnp.dot(a_exact, p, preferred_element_type=F32)
        acc = t if acc is None else acc + t
    return acc


def _seg_sum(x, bd):
    return _mm_xl(x, bd, 3)


def _sigmoid(x):
    return 1.0 / (1.0 + jnp.exp(-x))


def _silu(x):
    return x * _sigmoid(x)


def _softplus(x):
    return jnp.maximum(x, 0.0) + jnp.log1p(jnp.exp(-jnp.abs(x)))


def _gelu_tanh(x):
    return 0.5 * x * (1.0 + jnp.tanh(0.7978845608028654 * (x + 0.044715 * (x * x * x))))


def _rms(x, gain):
    return x * lax.rsqrt(jnp.mean(x * x, axis=-1, keepdims=True) + RMS_EPS) * gain


def _proj_kernel(x_ref, g_ref, w_ref, wdt_ref, bd_ref, gq_ref, gk_ref,
                 q_out, k_out, v_out, z_out, xbc_out, dt_out, c_out, d_out, *, attn_layout):
    h32 = _rms(x_ref[...], g_ref[...])
    h = h32.astype(BF16)
    bd = bd_ref[...]
    outs = (q_out, k_out, v_out, z_out, xbc_out, dt_out, c_out, d_out)
    lo = 0
    for idx, (width, out) in enumerate(zip(_P_WIDTHS, outs)):
        if out is dt_out:
            h_lo = (h32 - h.astype(F32)).astype(BF16)
            w_hi, w_lo = _parts(wdt_ref[...], 2)
            mm = lambda a, b: jnp.dot(a, b, preferred_element_type=F32)
            y = mm(h, w_hi) + (mm(h, w_lo) + mm(h_lo, w_hi))
        else:
            y = jnp.dot(h, w_ref[:, lo:lo + width], preferred_element_type=F32)
        if idx < 2:
            gain = (gq_ref, gk_ref)[idx][...]
            y = y * lax.rsqrt(_seg_sum(y * y, bd) * (1.0 / HEAD_DIM) + RMS_EPS) * gain
        if attn_layout and out is q_out:
            for hd in range(HEADS):
                q_out[0, hd] = y[:, hd * HEAD_DIM:(hd + 1) * HEAD_DIM]
        elif attn_layout and (out is k_out or out is v_out):
            out[0] = y.T
        else:
            out[...] = y
        lo += width


def _proj(x2, g, w, w_dt, bd, gq, gk, tm, seq, attn_layout):
    n = x2.shape[0]
    row = lambda wd: pl.BlockSpec((tm, wd), lambda i: (i, 0))
    out_specs = [row(wd) for wd in _P_WIDTHS]
    out_shape = [jax.ShapeDtypeStruct((n, wd), F32) for wd in _P_WIDTHS]
    if attn_layout:
        nb, per = n // seq, seq // tm
        out_specs[0] = pl.BlockSpec((1, HEADS, tm, HEAD_DIM), lambda i: (i // per, 0, i % per, 0))
        out_shape[0] = jax.ShapeDtypeStruct((nb, HEADS, seq, HEAD_DIM), F32)
        for j in (1, 2):
            out_specs[j] = pl.BlockSpec((1, GW, tm), lambda i: (i // per, 0, i % per))
            out_shape[j] = jax.ShapeDtypeStruct((nb, GW, seq), F32)
    return pl.pallas_call(
        functools.partial(_proj_kernel, attn_layout=attn_layout),
        grid=(n // tm,),
        in_specs=[row(D_MODEL),
                  _const_spec((1, D_MODEL)), _const_spec((D_MODEL, _P_TOTAL)), _const_spec((D_MODEL, LANE)),
                  _const_spec((GW, GW)), _const_spec((1, GW)), _const_spec((1, GW))],
        out_specs=out_specs,
        out_shape=out_shape,
        compiler_params=_cparams(("parallel",)),
        name="proj",
    )(x2, g, w, w_dt, bd, gq, gk)


def _sb_weights(neg_zs, readables, u, carry, cumsum_pieces):
    state = _sb_logs(neg_zs, readables, u, carry, cumsum_pieces)
    return _sb_finish(readables, *state)


def _sb_logs(neg_zs, readables, u, carry, cumsum_pieces):
    lks, lbs = [], []
    for nz, readable in zip(neg_zs, readables):
        lk = jnp.minimum(nz, 0.0) - jnp.log2(1.0 + jnp.exp2(-jnp.abs(nz)))
        lbs.append(lk - nz)
        lks.append(lk if readable is None else jnp.where(readable, lk, 0.0))
    tails = [_mm_xl(lk, u, cumsum_pieces) for lk in lks]
    scales = []
    for lk in lks:
        scales.append(jnp.exp2(carry))
        carry = carry + jnp.sum(lk, axis=1, keepdims=True)
    return lbs, tails, scales, carry


def _sb_finish(readables, lbs, tails, scales, carry):
    ws = []
    for readable, lb, tail in zip(readables, lbs, tails):
        w = jnp.exp2(lb + tail)
        ws.append(w if readable is None else jnp.where(readable, w, 0.0))
    return ws, scales, carry


def _attn_prompt_kernel(bias_ref, q_ref, k_ref, v_ref, u_ref, gain_ref, o_ref, acc_ref, cs_ref, nz_ref, *, tq, tk):
    h = pl.program_id(1)
    i = pl.program_id(2)
    neg_bias = bias_ref[h]
    u = u_ref[...]
    acc_ref[...] = jnp.zeros_like(acc_ref)
    cs_ref[...] = jnp.zeros_like(cs_ref)
    q = q_ref[0, 0].astype(BF16)
    cg = tq // tk
    key_minus_query = lax.broadcasted_iota(I32, (tq, tk), 1) - lax.broadcasted_iota(I32, (tq, tk), 0)

    def chunk_start(g, d):
        return pl.multiple_of(jnp.maximum(g, 0) * tq + (cg - 1 - d) * tk, tk)

    def logits(g):
        return [jnp.dot(q, k_ref[0, :, pl.ds(chunk_start(g, d), tk)].astype(BF16), preferred_element_type=F32)
                + neg_bias for d in range(cg)]

    def group(g, masked, nzs, acc, carry):
        starts = [chunk_start(g, d) for d in range(cg)]
        readables = [(key_minus_query < i * tq - ks) if masked else None for ks in starts]
        state = _sb_logs(nzs, readables, u, carry, 1)
        next_nzs = logits(g - 1)
        ws, scales, carry = _sb_finish(readables, *state)
        pvs = [lax.dot_general(w.astype(BF16), v_ref[0, :, pl.ds(ks, tk)].astype(BF16), (((1,), (1,)), ((), ())),
                               preferred_element_type=F32) for w, ks in zip(ws, starts)]
        for sc, pv in zip(scales, pvs):
            acc = acc + sc * pv
        return next_nzs, acc, carry

    def load_state():
        return [nz_ref[d] for d in range(cg)], acc_ref[...], cs_ref[...]

    def store_state(nzs, acc, carry):
        for d in range(cg):
            nz_ref[d] = nzs[d]
        acc_ref[...] = acc
        cs_ref[...] = carry

    store_state(*group(i, True, logits(i), jnp.zeros(acc_ref.shape, F32), jnp.zeros(cs_ref.shape, F32)))

    def past_pair(j, carry):
        g = i - 1 - 2 * j
        store_state(*group(g - 1, False, *group(g, False, *load_state())))
        return carry

    lax.fori_loop(0, i // 2, past_pair, 0)

    @pl.when(i % 2 == 1)
    def _():
        store_state(*group(0, False, *load_state()))

    o_ref[0, 0] = _rms(acc_ref[...], gain_ref[0])


def _attn_prompt(q, k, v, bias, gain, tq, tk):
    b, h, l, d = q.shape
    u = (jnp.arange(tk)[:, None] > jnp.arange(tk)[None, :]).astype(BF16)
    kern = functools.partial(_attn_prompt_kernel, tq=tq, tk=tk)
    return pl.pallas_call(
        kern,
        grid=(b, h, l // tq),
        in_specs=[pl.BlockSpec(memory_space=pltpu.SMEM),
                  pl.BlockSpec((1, 1, tq, d), lambda bi, hi, qi: (bi, hi, qi, 0)),
                  pl.BlockSpec((1, d, l), lambda bi, hi, qi: (bi, hi, 0)),
                  pl.BlockSpec((1, d, l), lambda bi, hi, qi: (bi, hi, 0)),
                  _const_spec((tk, tk)),
                  pl.BlockSpec((1, 1, d), lambda bi, hi, qi: (hi, 0, 0))],
        out_specs=pl.BlockSpec((1, 1, tq, d), lambda bi, hi, qi: (bi, hi, qi, 0)),
        out_shape=jax.ShapeDtypeStruct((b, h, l, d), F32),
        scratch_shapes=[pltpu.VMEM((tq, d), F32), pltpu.VMEM((tq, 1), F32), pltpu.VMEM((tq // tk, tq, tk), F32)],
        compiler_params=_cparams(("parallel", "parallel", "arbitrary")),
        name="attn_prompt",
    )(bias, q, k, v, u, gain)


def _attn_sample_kernel(pt_ref, q_ref, bias_ref, *refs, n_tok, n_group):
    kp_refs, vp_refs = refs[:n_group], refs[n_group:2 * n_group]
    kn_ref, vn_ref, u_ref, hm_ref, gain_ref, o_ref, acc_ref, cs_ref = refs[2 * n_group:]
    p = pl.program_id(1)
    rows = HEADS * n_tok
    q = q_ref[0].astype(BF16)

    def process(ks, vs, readables, acc, carry):
        neg_zs = [_mm(q, k_t) + bias_ref[...] for k_t in ks]
        ws, scales, carry = _sb_weights(neg_zs, readables, u_ref[...], carry, 2)
        for w, sc, v_t in zip(ws, scales, vs):
            acc = acc + sc * _mm_nt(w, v_t)
        return acc, carry

    @pl.when(p == 0)
    def _():
        tok = lax.rem(lax.broadcasted_iota(I32, (rows, PAGE), 0), n_tok)
        key = lax.broadcasted_iota(I32, (rows, PAGE), 1)
        acc, carry = process([kn_ref[0]], [vn_ref[0]], [key < tok],
                             jnp.zeros(acc_ref.shape, F32), jnp.zeros(cs_ref.shape, F32))
        acc_ref[...] = acc
        cs_ref[...] = carry

    order = list(reversed(range(n_group)))
    acc, carry = process([kp_refs[j][...] for j in order], [vp_refs[j][...] for j in order], [None] * n_group,
                         acc_ref[...], cs_ref[...])
    acc_ref[...] = acc
    cs_ref[...] = carry

    @pl.when(p == pl.num_programs(1) - 1)
    def _():
        o = acc_ref[...] * hm_ref[...]
        ss = jnp.sum(o * o, axis=1, keepdims=True) * (1.0 / HEAD_DIM)
        o_ref[0] = o * lax.rsqrt(ss + RMS_EPS) * gain_ref[...]


def _attn_sample(q16, bias16, cache_k, cache_v, layer, page_table, k_new, v_new, hm16, gain):
    s, rows, _ = q16.shape
    n_pages = page_table.shape[1]
    u = (jnp.arange(PAGE)[:, None] > jnp.arange(PAGE)[None, :]).astype(BF16)

    n_group = next(g for g in (16, 8, 4, 2, 1) if n_pages % g == 0)

    def page_spec(j):
        return pl.BlockSpec((None, None, GW, PAGE),
                            lambda si, p, pt: (layer, pt[si, n_pages - (p + 1) * n_group + j], 0, 0))

    page_specs = [page_spec(j) for j in range(n_group)]
    seq_spec = lambda shape: pl.BlockSpec(shape, lambda si, p, pt: (si, 0, 0))
    cst = lambda shape: pl.BlockSpec(shape, lambda si, p, pt: (0,) * len(shape))
    kern = functools.partial(_attn_sample_kernel, n_tok=rows // HEADS, n_group=n_group)
    return pl.pallas_call(
        kern,
        grid_spec=pltpu.PrefetchScalarGridSpec(
            num_scalar_prefetch=1,
            grid=(s, n_pages // n_group),
            in_specs=[seq_spec((1, rows, GW)), cst((rows, 1))] + page_specs + page_specs
            + [seq_spec((1, GW, PAGE)), seq_spec((1, GW, PAGE)),
               cst((PAGE, PAGE)), cst((rows, GW)), cst((1, GW))],
            out_specs=seq_spec((1, rows, GW)),
            scratch_shapes=[pltpu.VMEM((rows, GW), F32), pltpu.VMEM((rows, 1), F32)]),
        out_shape=jax.ShapeDtypeStruct((s, rows, GW), F32),
        compiler_params=_cparams(("parallel", "arbitrary")),
        name="attn_sample",
    )(page_table, q16, bias16, *([cache_k] * n_group), *([cache_v] * n_group), k_new, v_new, u, hm16, gain)


def _ssd_kernel(z_ref, x_ref, dt_ref, c0_ref, s0_ref, cw_ref, cb_ref, dtb_ref, a_ref, dsk_ref, nrm_ref, lt_ref,
                y_ref, st_out, xf_ref, st_ref, *, n_valid):
    t = CHUNK
    c = pl.program_id(1)

    @pl.when(c == 0)
    def _():
        xf_ref[0:SUBLANE, :] = c0_ref[0]
        st_ref[...] = s0_ref[0]

    xf_ref[SUBLANE:SUBLANE + t, :] = x_ref[0]
    conv = cb_ref[...]
    for i in range(SSM_CONV):
        conv = conv + cw_ref[i:i + 1, :] * xf_ref[pl.ds(SUBLANE - (SSM_CONV - 1) + i, t), :]
    xf_ref[0:SUBLANE, :] = xf_ref[t:t + SUBLANE, :]
    xc = _silu(conv)
    xs = xc[:, :GW]
    bm = xc[:, GW:GW + SSM_GROUPS * SSM_STATE]
    cm = xc[:, GW + SSM_GROUPS * SSM_STATE:]

    dt = _softplus(dt_ref[0] + dtb_ref[...])
    if n_valid < t:
        dt = jnp.where(lax.broadcasted_iota(I32, dt.shape, 0) < n_valid, dt, 0.0)
    acum = _mm_xr(lt_ref[...], dt * a_ref[...], 3)
    acum_t = acum.T
    dt_t = dt.T
    a_last = acum[t - 1:t, :]

    row = lax.broadcasted_iota(I32, (t, t), 0)
    col = lax.broadcasted_iota(I32, (t, t), 1)
    causal = row >= col
    lane = lax.broadcasted_iota(I32, (1, GW), 1)
    gmat = [_mm_nt(cm[:, g * SSM_STATE:(g + 1) * SSM_STATE], bm[:, g * SSM_STATE:(g + 1) * SSM_STATE])
            for g in range(SSM_GROUPS)]
    st_prev = st_ref[...]
    y = jnp.zeros((t, GW), F32)
    st_new = jnp.zeros((SSM_STATE, GW), F32)
    e_acum = jnp.zeros((t, GW), F32)
    a_end = jnp.zeros((1, GW), F32)
    for h in range(HEADS):
        g = h // (HEADS // SSM_GROUPS)
        mh = (lane // HEAD_DIM == h).astype(F32)
        col_h = acum[:, h:h + 1]
        decay = jnp.exp(jnp.where(causal, col_h - acum_t[h:h + 1, :], -1e30))
        scores = gmat[g] * decay * dt_t[h:h + 1, :]
        xm = xs * mh
        y = y + _mm(scores, xm)
        to_end = jnp.exp(a_last[:, h:h + 1] - col_h) * dt[:, h:h + 1]
        bw = bm[:, g * SSM_STATE:(g + 1) * SSM_STATE] * to_end
        st_new = st_new + _mm(bw.T, xm)
        e_acum = e_acum + mh * jnp.exp(col_h)
        a_end = a_end + mh * a_last[:, h:h + 1]
    y_in = jnp.where(lane < GW // SSM_GROUPS, _mm(cm[:, :SSM_STATE], st_prev), _mm(cm[:, SSM_STATE:], st_prev))
    st = st_prev * jnp.exp(a_end) + st_new
    st_ref[...] = st
    st_out[0] = st
    y = y + y_in * e_acum + dsk_ref[...] * xs
    y = y * _silu(z_ref[0])
    y_ref[0] = _rms(y, nrm_ref[...])


def _ssd(z, xbc, dt, conv0, st0, cw, cb, dtb, a_neg, dsk, nrm, n_valid):
    b, l, _ = z.shape
    lt = (jnp.arange(CHUNK)[:, None] >= jnp.arange(CHUNK)[None, :]).astype(BF16)
    blk = lambda wd: pl.BlockSpec((1, CHUNK, wd), lambda bi, ci: (bi, ci, 0))
    per_b = lambda shape: pl.BlockSpec(shape, lambda bi, ci: (bi, 0, 0))
    kern = functools.partial(_ssd_kernel, n_valid=n_valid)
    return pl.pallas_call(
        kern,
        grid=(b, l // CHUNK),
        in_specs=[blk(GW), blk(SSM_CONV_DIM), blk(LANE),
                  per_b((1, SUBLANE, SSM_CONV_DIM)), per_b((1, SSM_STATE, GW)),
                  _const_spec((SSM_CONV, SSM_CONV_DIM)), _const_spec((1, SSM_CONV_DIM)),
                  _const_spec((1, LANE)), _const_spec((1, LANE)), _const_spec((1, GW)), _const_spec((1, GW)),
                  _const_spec((CHUNK, CHUNK))],
        out_specs=[blk(GW), per_b((1, SSM_STATE, GW))],
        out_shape=[jax.ShapeDtypeStruct((b, l, GW), F32), jax.ShapeDtypeStruct((b, SSM_STATE, GW), F32)],
        scratch_shapes=[pltpu.VMEM((CHUNK + SUBLANE, SSM_CONV_DIM), F32), pltpu.VMEM((SSM_STATE, GW), F32)],
        compiler_params=_cparams(("parallel", "arbitrary")),
        name="ssd",
    )(z, xbc, dt, conv0, st0, cw, cb, dtb, a_neg, dsk, nrm, lt)


def _rwkv_kernel(p_ref, sh0_ref, s0_ref, mu_ref, w0_ref, a0_ref, dup_ref, iup_ref, gup_ref,
                 kk_ref, ka_ref, rk_ref, gnw_ref, gnb_ref, bd_ref, lt_ref,
                 y_ref, s_out, pf_ref, st_ref, *, bb, t, n_sub, n_valid):
    c = pl.program_id(1)
    bd = bd_ref[...]
    lane = lax.broadcasted_iota(I32, (1, GW), 1)
    head_mask = [(lane // HEAD_DIM == h).astype(F32) for h in range(HEADS)]
    stack = lambda x: jnp.concatenate([x * m for m in head_mask], axis=0)
    unstack = lambda x: sum(x[h * t:(h + 1) * t] for h in range(HEADS))
    step_r = lax.broadcasted_iota(I32, (HEADS * t, HEADS * t), 0) % t
    step_c = lax.broadcasted_iota(I32, (HEADS * t, HEADS * t), 1) % t
    nt = lambda x, y: lax.dot_general(x, y, (((1,), (1,)), ((), ())), preferred_element_type=F32)
    tn = lambda x, y: lax.dot_general(x, y, (((0,), (0,)), ((), ())), preferred_element_type=F32)
    mm = lambda x, y: jnp.dot(x, y, preferred_element_type=F32)

    @pl.when(c == 0)
    def _():
        for b in range(bb):
            pf_ref[b, 0:SUBLANE, :] = sh0_ref[b]
            st_ref[b] = stack(s0_ref[b])

    tt = n_sub * t
    subs = [[] for _ in range(n_sub)]
    for b in range(bb):
        pf_ref[b, SUBLANE:SUBLANE + tt, :] = p_ref[b]
        cur = p_ref[b]
        prev = pf_ref[b, pl.ds(SUBLANE - 1, tt), :]
        pf_ref[b, 0:SUBLANE, :] = pf_ref[b, tt:tt + SUBLANE, :]
        xs = cur + (prev - cur) * mu_ref[...]
        r = xs[:, 0:GW]
        k = xs[:, GW:2 * GW]
        v = xs[:, 2 * GW:3 * GW]
        lo = xs[:, RWKV_LORA_LO:RWKV_PROJ]
        w_log = -_softplus(-(w0_ref[...] + _mm(jnp.tanh(lo), dup_ref[...]))) - 0.5
        a = _sigmoid(a0_ref[...] + _mm(lo, iup_ref[...]))
        kkr = k * kk_ref[...]
        kk = kkr / jnp.maximum(jnp.sqrt(_seg_sum(kkr * kkr, bd)), 1e-12)
        kmod = k * (1.0 + (a - 1.0) * ka_ref[...])
        ka = kk * a
        log_w = -jnp.exp(w_log)
        if n_valid is not None:
            valid = c * tt + lax.broadcasted_iota(I32, (tt, GW), 0) < n_valid
            log_w = jnp.where(valid, log_w, 0.0)
            kk, ka, kmod, v = (jnp.where(valid, x, 0.0) for x in (kk, ka, kmod, v))
        bonus = _seg_sum(r * kmod * rk_ref[...], bd) * v
        gate = _mm(_sigmoid(lo), gup_ref[...])

        for sub in range(n_sub):
            rows = slice(sub * t, (sub + 1) * t)
            lw = log_w[rows]
            cum = _mm_xr(lt_ref[...], lw, 3)
            inv_g = jnp.exp(-cum)
            subs[sub].append(dict(
                a=stack(kk[rows] * jnp.exp(cum - lw)).astype(BF16), b=stack(ka[rows] * inv_g).astype(BF16),
                k=stack(kmod[rows] * inv_g).astype(BF16), r=stack(r[rows] * jnp.exp(cum)).astype(BF16),
                v=stack(v[rows]).astype(BF16), g_end=jnp.exp(cum[t - 1:t, :]),
                bonus=bonus[rows], gate=gate[rows]))

    strict = step_r > step_c
    incl = step_r >= step_c
    states = [st_ref[b] for b in range(bb)]
    for sub, seqs in enumerate(subs):
        for s, s0 in zip(seqs, states):
            s["s0"] = s0
            s["s0_b"] = s0.astype(BF16)
            s["power"] = jnp.where(strict, -nt(s["a"], s["b"]), 0.0).astype(BF16)
            s["l_ak"] = jnp.where(strict, nt(s["a"], s["k"]), 0.0).astype(BF16)
        for s in seqs:
            s["u"] = nt(s["a"], s["s0_b"]) + mm(s["l_ak"], s["v"])
        for s in seqs:
            s["l_rk"] = jnp.where(incl, nt(s["r"], s["k"]), 0.0).astype(BF16)
            s["l_rb"] = jnp.where(incl, nt(s["r"], s["b"]), 0.0).astype(BF16)
        for s in seqs:
            s["u"] = s["u"] + mm(s["power"], s["u"].astype(BF16))
        for _ in range(t.bit_length() - 2):
            for s in seqs:
                s["power"] = mm(s["power"], s["power"]).astype(BF16)
            for s in seqs:
                s["u"] = s["u"] + mm(s["power"], s["u"].astype(BF16))
        for s in seqs:
            s["u_b"] = s["u"].astype(BF16)
            s["y"] = unstack(nt(s["r"], s["s0_b"]) + mm(s["l_rk"], s["v"]) - mm(s["l_rb"], s["u_b"]))
        states = [(s["s0"] + tn(s["v"], s["k"]) - tn(s["u_b"], s["b"])) * s["g_end"] for s in seqs]

        for b, s in enumerate(seqs):
            y = s["y"]
            mean = _seg_sum(y, bd) * (1.0 / HEAD_DIM)
            yc = y - mean
            var = _seg_sum(yc * yc, bd) * (1.0 / HEAD_DIM)
            yn = yc * lax.rsqrt(var + RWKV_GN_EPS) * gnw_ref[...] + gnb_ref[...]
            y_ref[b, sub * t:(sub + 1) * t, :] = (yn + s["bonus"]) * s["gate"]

    for b in range(bb):
        st_ref[b] = states[b]
        s_out[b] = unstack(states[b])


def _rwkv(pc, sh0, s0, mu, w0, a0, dup, iup, gup, k_k, k_a, r_k, gnw, gnb, bd, bb, n_valid):
    b, l, _ = pc.shape
    t = HEAD_DIM
    lt = (jnp.arange(t)[:, None] >= jnp.arange(t)[None, :]).astype(BF16)
    n_sub = 2 if l % (2 * t) == 0 else 1
    tt = n_sub * t
    blk = lambda wd: pl.BlockSpec((bb, tt, wd), lambda bi, ci: (bi, ci, 0))
    per_b = lambda shape: pl.BlockSpec(shape, lambda bi, ci: (bi, 0, 0))
    vec = lambda wd: _const_spec((1, wd))
    kern = functools.partial(_rwkv_kernel, bb=bb, t=t, n_sub=n_sub, n_valid=n_valid if n_valid < l else None)
    return pl.pallas_call(
        kern,
        grid=(b // bb, l // tt),
        in_specs=[blk(RWKV_PROJ), per_b((bb, SUBLANE, RWKV_PROJ)), per_b((bb, HEAD_DIM, GW)),
                  vec(RWKV_PROJ), vec(GW), vec(GW),
                  _const_spec((RWKV_LORA_W, GW)), _const_spec((RWKV_LORA_W, GW)), _const_spec((RWKV_LORA_W, GW)),
                  vec(GW), vec(GW), vec(GW), vec(GW), vec(GW),
                  _const_spec((GW, GW)), _const_spec((t, t))],
        out_specs=[blk(GW), per_b((bb, HEAD_DIM, GW))],
        out_shape=[jax.ShapeDtypeStruct((b, l, GW), F32), jax.ShapeDtypeStruct((b, HEAD_DIM, GW), F32)],
        scratch_shapes=[pltpu.VMEM((bb, tt + SUBLANE, RWKV_PROJ), F32), pltpu.VMEM((bb, HEADS * HEAD_DIM, GW), F32)],
        compiler_params=_cparams(("parallel", "arbitrary")),
        name="rwkv",
    )(pc, sh0, s0, mu, w0, a0, dup, iup, gup, k_k, k_a, r_k, gnw, gnb, bd, lt)


def _sgu_kernel(p_ref, w_ref, bias_ref, gv_ref, bd_ref, o_ref, v_out, *, n_sub):
    t = CHUNK
    p = p_ref[0]
    u = _gelu_tanh(p[:, :GW])
    v = _gelu_tanh(p[:, GW:])
    vn = v * lax.rsqrt(_seg_sum(v * v, bd_ref[...]) * (1.0 / HEAD_DIM) + RMS_EPS) * gv_ref[...]
    v_out[0] = vn
    causal = lax.broadcasted_iota(I32, (t, t), 0) >= lax.broadcasted_iota(I32, (t, t), 1)
    lane = lax.broadcasted_iota(I32, (1, GW), 1)
    w_tril = [jnp.where(causal, w_ref[g], 0.0).astype(BF16) for g in range(HEADS)]
    head = [(lane // HEAD_DIM == g).astype(F32) for g in range(HEADS)]
    for s in range(n_sub):
        rows = slice(s * t, (s + 1) * t)
        mixed = bias_ref[...]
        for g in range(HEADS):
            mixed = mixed + _mm(w_tril[g], vn[rows] * head[g])
        o_ref[0, rows, :] = u[rows] * mixed


def _sgu(pd, w, bias, gv, bd):
    b, l, _ = pd.shape
    n_sub = next(s for s in (4, 2, 1) if l % (s * CHUNK) == 0)
    rows = n_sub * CHUNK
    return pl.pallas_call(
        functools.partial(_sgu_kernel, n_sub=n_sub),
        grid=(b, l // rows),
        in_specs=[pl.BlockSpec((1, rows, 2 * GW), lambda bi, ci: (bi, ci, 0)),
                  _const_spec((HEADS, CHUNK, CHUNK)), _const_spec((CHUNK, GW)), _const_spec((1, GW)),
                  _const_spec((GW, GW))],
        out_specs=[pl.BlockSpec((1, rows, GW), lambda bi, ci: (bi, ci, 0))] * 2,
        out_shape=[jax.ShapeDtypeStruct((b, l, GW), F32)] * 2,
        compiler_params=_cparams(("parallel", "parallel")),
        name="sgu",
    )(pd, w, bias, gv, bd)


def _outproj_kernel(x_ref, oa_ref, ob_ref, oc_ref, od_ref, w_ref, o_ref, *, oa_heads):
    acc = x_ref[...]
    if oa_heads:
        for hd in range(HEADS):
            acc = acc + jnp.dot(oa_ref[0, hd].astype(BF16), w_ref[hd * HEAD_DIM:(hd + 1) * HEAD_DIM, :],
                                preferred_element_type=F32)
    for i, r in enumerate((oa_ref, ob_ref, oc_ref, od_ref)):
        if i > 0 or not oa_heads:
            acc = acc + jnp.dot(r[...].astype(BF16), w_ref[i * GW:(i + 1) * GW, :], preferred_element_type=F32)
    o_ref[...] = acc


def _outproj(x2, oa, ob, oc, od, w, tm):
    n = x2.shape[0]
    row = lambda wd: pl.BlockSpec((tm, wd), lambda i: (i, 0))
    oa_heads = oa.ndim == 4
    if oa_heads:
        per = oa.shape[2] // tm
        oa_spec = pl.BlockSpec((1, HEADS, tm, HEAD_DIM), lambda i: (i // per, 0, i % per, 0))
    else:
        oa_spec = row(GW)
    return pl.pallas_call(
        functools.partial(_outproj_kernel, oa_heads=oa_heads),
        grid=(n // tm,),
        in_specs=[row(D_MODEL), oa_spec, row(GW), row(GW), row(GW), _const_spec((4 * GW, D_MODEL))],
        out_specs=row(D_MODEL),
        out_shape=jax.ShapeDtypeStruct((n, D_MODEL), F32),
        compiler_params=_cparams(("parallel",)),
        name="outproj",
    )(x2, oa, ob, oc, od, w)


def _ffn_kernel(x_ref, g_ref, wg_ref, wu_ref, wd_ref, o_ref, h_s):
    f = pl.program_id(1)

    @pl.when(f == 0)
    def _():
        x = x_ref[...]
        h_s[...] = _rms(x, g_ref[...]).astype(BF16)
        o_ref[...] = x

    h = h_s[...]
    a = jnp.dot(h, wg_ref[...], preferred_element_type=F32)
    u = jnp.dot(h, wu_ref[...], preferred_element_type=F32)
    o_ref[...] += jnp.dot((_silu(a) * u).astype(BF16), wd_ref[...], preferred_element_type=F32)


def _ffn(x2, g, wg, wu, wd, tm, tf):
    n = x2.shape[0]
    d_ff = wg.shape[1]
    return pl.pallas_call(
        _ffn_kernel,
        grid=(n // tm, d_ff // tf),
        in_specs=[pl.BlockSpec((tm, D_MODEL), lambda i, f: (i, 0)), _const_spec((1, D_MODEL)),
                  pl.BlockSpec((D_MODEL, tf), lambda i, f: (0, f)),
                  pl.BlockSpec((D_MODEL, tf), lambda i, f: (0, f)),
                  pl.BlockSpec((tf, D_MODEL), lambda i, f: (f, 0))],
        out_specs=pl.BlockSpec((tm, D_MODEL), lambda i, f: (i, 0)),
        out_shape=jax.ShapeDtypeStruct((n, D_MODEL), F32),
        scratch_shapes=[pltpu.VMEM((tm, D_MODEL), BF16)],
        compiler_params=_cparams(("parallel", "arbitrary")),
        name="ffn",
    )(x2, g, wg, wu, wd)


def _router_kernel(x_ref, g_ref, rwt_ref, idx_ref, gate_ref):
    h = _rms(x_ref[...], g_ref[...])
    hh, hl = _parts(h, 2)
    wh, wl = _parts(rwt_ref[...], 2)
    nt = lambda a, b: lax.dot_general(a, b, (((1,), (1,)), ((), ())), preferred_element_type=F32)
    logits = nt(wh, hh) + (nt(wh, hl) + nt(wl, hh))
    e_id = lax.broadcasted_iota(I32, logits.shape, 0)
    m1 = jnp.max(logits, axis=0, keepdims=True)
    i1 = jnp.min(jnp.where(logits == m1, e_id, N_EXPERTS), axis=0, keepdims=True)
    rest = jnp.where(e_id == i1, -jnp.inf, logits)
    m2 = jnp.max(rest, axis=0, keepdims=True)
    i2 = jnp.min(jnp.where(rest == m2, e_id, N_EXPERTS), axis=0, keepdims=True)
    e = jnp.exp(m2 - m1)
    g1 = 1.0 / (1.0 + e)
    idx_ref[...] = jnp.where(e_id == 0, i1, jnp.where(e_id == 1, i2, 0))
    gate_ref[...] = jnp.where(e_id == 0, g1, jnp.where(e_id == 1, e * g1, 0.0))


def _router(x2, g, rwt, tm):
    n = x2.shape[0]
    return pl.pallas_call(
        _router_kernel,
        grid=(n // tm,),
        in_specs=[pl.BlockSpec((tm, D_MODEL), lambda i: (i, 0)), _const_spec((1, D_MODEL)),
                  _const_spec((N_EXPERTS, D_MODEL))],
        out_specs=[pl.BlockSpec((N_EXPERTS, tm), lambda i: (0, i))] * 2,
        out_shape=[jax.ShapeDtypeStruct((N_EXPERTS, n), I32), jax.ShapeDtypeStruct((N_EXPERTS, n), F32)],
        compiler_params=_cparams(("parallel",)),
        name="router",
    )(x2, g, rwt)


def _moe_kernel(tok_ref, off_ref, cnt_ref, gts_ref, x_ref, g_ref, wg_ref, wu_ref, wd_ref, o_ref,
                xg_s, og_s, *, gr):
    b = pl.program_id(0)
    e = pl.program_id(1)
    f = pl.program_id(2)
    n = cnt_ref[b * N_EXPERTS + e]
    o0 = off_ref[b * N_EXPERTS + e]

    @pl.when((e == 0) & (f == 0))
    def _():
        o_ref[...] = x_ref[...]

    @pl.when((b == 0) & (e == 0) & (f == 0))
    def _():
        xg_s[...] = jnp.zeros_like(xg_s)

    def row_loop(body, unroll=4):
        def main(i, carry):
            for r in range(unroll):
                body(i * unroll + r)
            return carry

        def tail(i, carry):
            body(i)
            return carry
        lax.fori_loop(0, n // unroll, main, 0)
        lax.fori_loop((n // unroll) * unroll, n, tail, 0)

    @pl.when(f == 0)
    def _():
        def gather(i):
            xg_s[pl.ds(i, 1), :] = x_ref[pl.ds(tok_ref[o0 + i], 1), :]
        row_loop(gather)

    def group(gi, carry):
        r0 = pl.multiple_of(gi * gr, gr)

        @pl.when(f == 0)
        def _():
            xg_s[pl.ds(r0, gr), :] = _rms(xg_s[pl.ds(r0, gr), :], g_ref[...])

        xb = xg_s[pl.ds(r0, gr), :].astype(BF16)
        a = jnp.dot(xb, wg_ref[...], preferred_element_type=F32)
        u = jnp.dot(xb, wu_ref[...], preferred_element_type=F32)
        y = jnp.dot((_silu(a) * u).astype(BF16), wd_ref[...], preferred_element_type=F32)

        @pl.when(f == 0)
        def _():
            og_s[pl.ds(r0, gr), :] = y

        @pl.when(f > 0)
        def _():
            og_s[pl.ds(r0, gr), :] += y
        return carry

    lax.fori_loop(0, (n + gr - 1) // gr, group, 0)

    @pl.when(f == pl.num_programs(2) - 1)
    def _():
        def scatter(i):
            tk = tok_ref[o0 + i]
            o_ref[pl.ds(tk, 1), :] += gts_ref[o0 + i] * og_s[pl.ds(i, 1), :]
        row_loop(scatter)


def _moe(x2, g, tok, off, cnt, gts, wg, wu, wd, tm, gr):
    n = x2.shape[0]
    n_f, tf = MOE_FF_BLOCKS, wg.shape[2] // MOE_FF_BLOCKS
    kern = functools.partial(_moe_kernel, gr=gr)
    return pl.pallas_call(
        kern,
        grid_spec=pltpu.PrefetchScalarGridSpec(
            num_scalar_prefetch=3,
            grid=(n // tm, N_EXPERTS, n_f),
            in_specs=[pl.BlockSpec(memory_space=pltpu.SMEM),
                      pl.BlockSpec((tm, D_MODEL), lambda b, e, f, *_: (b, 0), pipeline_mode=pl.Buffered(1)),
                      pl.BlockSpec((1, D_MODEL), lambda b, e, f, *_: (0, 0)),
                      pl.BlockSpec((None, D_MODEL, tf), lambda b, e, f, *_: (e, 0, f)),
                      pl.BlockSpec((None, D_MODEL, tf), lambda b, e, f, *_: (e, 0, f)),
                      pl.BlockSpec((None, tf, D_MODEL), lambda b, e, f, *_: (e, f, 0))],
            out_specs=pl.BlockSpec((tm, D_MODEL), lambda b, e, f, *_: (b, 0)),
            scratch_shapes=[pltpu.VMEM((-(-tm // gr) * gr, D_MODEL), F32),
                            pltpu.VMEM((-(-tm // gr) * gr, D_MODEL), F32)]),
        out_shape=jax.ShapeDtypeStruct((n, D_MODEL), F32),
        compiler_params=_cparams(("arbitrary", "arbitrary", "arbitrary")),
        name="moe",
    )(tok, off, cnt, gts, x2, g, wg, wu, wd)


def _moe_group_rows(tm):
    return min(tm, -(-(9 * tm // 32) // 32) * 32)


def _moe_plan(idx, gate, tm):
    n = idx.shape[1]
    nb = n // tm
    e = idx[:2].T.reshape(nb, 2 * tm)
    gt = gate[:2].T.reshape(nb, 2 * tm)
    order = jnp.argsort(e, axis=1, stable=True)
    tok = (order // 2).astype(I32)
    gts = jnp.take_along_axis(gt, order, axis=1)
    cnt = jnp.sum(e[:, :, None] == jnp.arange(N_EXPERTS, dtype=I32)[None, None, :], axis=1).astype(I32)
    off = jnp.cumsum(cnt, axis=1) - cnt + (jnp.arange(nb, dtype=I32) * (2 * tm))[:, None]
    return tok.reshape(-1), off.reshape(-1).astype(I32), cnt.reshape(-1), gts.reshape(-1)


def _layer(x, l, p, consts, past, cache_k, cache_v, page_table):
    b, seq, _ = x.shape
    n = b * seq
    is_sample = past is not None
    tm = min(n, 512)
    x2 = x.reshape(n, D_MODEL)
    bd, hm = consts["bd"], consts["hm"]

    qn, kn, v, z, xbc, dtr, pc, pd = _proj(x2, p["norm_mix"], p["w_in"], p["w_dt"], bd, p["gq"], p["gk"], tm,
                                           seq, not is_sample)

    if not is_sample:
        from_feature_major = lambda a: a.reshape(b, HEADS, HEAD_DIM, seq).transpose(0, 3, 1, 2)
        k_new, v_new = from_feature_major(kn), from_feature_major(v)
        oa = _attn_prompt(qn, kn, v, p["sb_bias"], p["sb_out_norm"].reshape(HEADS, 1, HEAD_DIM),
                          min(seq, ATTN_TQ), min(seq, ATTN_TK))
    else:
        k_new = kn.reshape(b, seq, HEADS, HEAD_DIM)
        v_new = v.reshape(b, seq, HEADS, HEAD_DIM)
        rows = HEADS * seq
        q16 = (qn.reshape(b, 1, seq, GW) * hm[None, :, None, :]).reshape(b, rows, GW)
        pad_keys_t = lambda a: jnp.pad(a.reshape(b, seq, GW), ((0, 0), (0, PAGE - seq), (0, 0))).transpose(0, 2, 1)
        o16 = _attn_sample(q16, jnp.repeat(p["sb_bias"], seq).reshape(rows, 1), cache_k, cache_v, l, page_table,
                           pad_keys_t(kn), pad_keys_t(v), jnp.repeat(hm, seq, axis=0),
                           p["sb_out_norm"].reshape(1, GW))
        oa = o16.reshape(b, HEADS, seq, GW).sum(axis=1).reshape(n, GW)

    lc = -(-seq // CHUNK) * CHUNK
    pad_seq = lambda a, to: jnp.pad(a.reshape(b, seq, -1), ((0, 0), (0, to - seq), (0, 0)))

    xbc3 = xbc.reshape(b, seq, SSM_CONV_DIM)
    if is_sample:
        conv0 = past["conv"]
        st0 = past["ssm"].reshape(b, GW, SSM_STATE).transpose(0, 2, 1)
    else:
        conv0 = jnp.zeros((b, SSM_CONV - 1, SSM_CONV_DIM), F32)
        st0 = jnp.zeros((b, SSM_STATE, GW), F32)
    conv8 = jnp.pad(conv0, ((0, 0), (SUBLANE - (SSM_CONV - 1), 0), (0, 0)))
    ob, st = _ssd(pad_seq(z, lc), pad_seq(xbc, lc), pad_seq(dtr, lc), conv8, st0,
                  p["conv_w"], p["conv_b"], p["dt_bias"], p["a_neg"], p["d_skip"], p["ssm_norm"], min(seq, CHUNK))
    ob = ob[:, :seq].reshape(n, GW)
    ssm_new = st.transpose(0, 2, 1).reshape(b, HEADS, HEAD_DIM, SSM_STATE)
    conv_new = jnp.concatenate([conv0, xbc3], axis=1)[:, seq:]

    pc3 = pc.reshape(b, seq, RWKV_PROJ)
    if is_sample:
        shift0 = past["shift"]
        wkv0 = past["wkv"].transpose(0, 2, 1, 3).reshape(b, HEAD_DIM, GW)
    else:
        shift0 = jnp.zeros((b, RWKV_PROJ), F32)
        wkv0 = jnp.zeros((b, HEAD_DIM, GW), F32)
    sh8 = jnp.pad(shift0[:, None, :], ((0, 0), (SUBLANE - 1, 0), (0, 0)))
    lr = -(-seq // HEAD_DIM) * HEAD_DIM
    oc, wkv = _rwkv(pad_seq(pc, lr), sh8, wkv0, p["shift_mu"], p["decay_w0"], p["iclr_a0"],
                    p["decay_up"], p["iclr_up"], p["gate_up"], p["k_k"], p["k_a"], p["r_k"], p["gn_w"], p["gn_b"],
                    bd, 4 if b % 4 == 0 else 2, seq)
    oc = oc[:, :seq].reshape(n, GW)
    wkv_new = wkv.reshape(b, HEAD_DIM, HEADS, HEAD_DIM).transpose(0, 2, 1, 3)
    shift_new = pc3[:, -1]

    od, sgu_v = _sgu(pad_seq(pd, lc), p["sgu_w"], p["sgu_bias"], p["sgu_v_norm"], bd)
    od = od[:, :seq].reshape(n, GW)
    sgu_v = sgu_v[:, :seq].reshape(b, seq, HEADS, HEAD_DIM)

    x_mid = _outproj(x2, oa, ob, oc, od, p["w_out"], tm)

    if "ffn_wg" in p:
        tmf = min(n, 512)
        x_out = _ffn(x_mid, p["norm_ffn"], p["ffn_wg"], p["ffn_wu"], p["ffn_wd"], tmf, p["ffn_wg"].shape[1] // 2)
    else:
        tmm = min(n, 1024)
        idx, gate = _router(x_mid, p["norm_ffn"], p["router_wt"], min(n, 512))
        tok, off, cnt, gts = _moe_plan(idx, gate, tmm)
        x_out = _moe(x_mid, p["norm_ffn"], tok, off, cnt, gts, p["moe_wg"], p["moe_wu"], p["moe_wd"],
                     tmm, _moe_group_rows(tmm))
    return x_out.reshape(b, seq, D_MODEL), (k_new, v_new, ssm_new, conv_new, wkv_new, shift_new, sgu_v)


def _trunk(x, layers, consts, pasts, cache_k, cache_v, page_table):
    states = []
    for l, p in enumerate(layers):
        x, st = _layer(x, l, p, consts, None if pasts is None else pasts[l], cache_k, cache_v, page_table)
        states.append(st)
    return x, [jnp.stack(s) for s in zip(*states)]


def _pad_rows(w, lo, total):
    return jnp.pad(w, ((lo, total - lo - w.shape[0]), (0, 0)))


def kernel(x_prompt, x_sample, cache_k, cache_v, page_table, state_ssm, state_conv, state_wkv, state_shift, norm_mix, norm_ffn, w_in, w_out, q_norm, k_norm, sb_bias, sb_out_norm, conv_w, conv_b, dt_bias, a_log, d_skip, ssm_norm, shift_mu, decay_w0, decay_up, iclr_a0, iclr_up, gate_up, k_k, k_a, r_k, gn_w, gn_b, sgu_w, sgu_b, sgu_v_norm, ffn_wg, ffn_wu, ffn_wd, router_w, moe_wg, moe_wu, moe_wd):
    depth = w_in.shape[0]
    lane = jnp.arange(GW)
    consts = {
        "bd": (lane[:, None] // HEAD_DIM == lane[None, :] // HEAD_DIM).astype(BF16),
        "hm": (jnp.arange(HEADS)[:, None] == lane[None, :] // HEAD_DIM).astype(F32),
    }
    dt_col = 4 * GW + SSM_CONV_DIM
    row = lambda a: a.reshape(1, -1).astype(F32)
    layers = []
    for l in range(depth):
        w = w_in[l]
        w_packed = jnp.concatenate(
            [w[:, :dt_col + HEADS], jnp.zeros((D_MODEL, LANE - HEADS), F32), w[:, dt_col + HEADS:]], axis=1)
        p = dict(
            norm_mix=row(norm_mix[l]), norm_ffn=row(norm_ffn[l]),
            w_in=w_packed.astype(BF16), w_dt=w_packed[:, dt_col:dt_col + LANE], w_out=w_out[l].astype(BF16),
            gq=row(jnp.tile(q_norm[l], HEADS)) * -(LOG2E * HEAD_DIM ** -0.5), gk=row(jnp.tile(k_norm[l], HEADS)),
            sb_bias=sb_bias[l].astype(F32) * -LOG2E, sb_out_norm=sb_out_norm[l],
            conv_w=conv_w[l], conv_b=row(conv_b[l]),
            dt_bias=jnp.pad(row(dt_bias[l]), ((0, 0), (0, LANE - HEADS))),
            a_neg=jnp.pad(row(-jnp.exp(a_log[l])), ((0, 0), (0, LANE - HEADS))),
            d_skip=row(jnp.repeat(d_skip[l], HEAD_DIM)), ssm_norm=row(ssm_norm[l]),
            shift_mu=row(shift_mu[l]), decay_w0=row(decay_w0[l]), iclr_a0=row(iclr_a0[l]),
            decay_up=_pad_rows(decay_up[l], 0, RWKV_LORA_W).astype(BF16),
            iclr_up=_pad_rows(iclr_up[l], DECAY_LORA, RWKV_LORA_W).astype(BF16),
            gate_up=_pad_rows(gate_up[l], DECAY_LORA + ICLR_LORA, RWKV_LORA_W).astype(BF16),
            k_k=row(k_k[l]), k_a=row(k_a[l]), r_k=row(r_k[l]), gn_w=row(gn_w[l]), gn_b=row(gn_b[l]),
            sgu_w=sgu_w[l][:, :CHUNK, :CHUNK],
            sgu_bias=jnp.repeat(sgu_b[l][:, :CHUNK].T, HEAD_DIM, axis=1),
            sgu_v_norm=row(sgu_v_norm[l]),
        )
        if l % 2 == 0:
            p.update(ffn_wg=ffn_wg[l // 2].astype(BF16), ffn_wu=ffn_wu[l // 2].astype(BF16),
                     ffn_wd=ffn_wd[l // 2].astype(BF16))
        else:
            p.update(router_wt=router_w[l // 2].T, moe_wg=moe_wg[l // 2].astype(BF16),
                     moe_wu=moe_wu[l // 2].astype(BF16), moe_wd=moe_wd[l // 2].astype(BF16))
        layers.append(p)

    to_pages_t = lambda c: c.transpose(0, 1, 3, 4, 2).reshape(c.shape[0], c.shape[1], GW, PAGE)
    ck, cv = to_pages_t(cache_k), to_pages_t(cache_v)
    pasts = [dict(ssm=state_ssm[l], conv=state_conv[l], wkv=state_wkv[l], shift=state_shift[l])
             for l in range(depth)]

    y_p, (k_p, v_p, ssm_p, conv_p, wkv_p, shift_p, _) = _trunk(x_prompt, layers, consts, None, ck, cv, page_table)
    y_s, (k_s, v_s, ssm_s, conv_s, wkv_s, shift_s, sgu_v_s) = _trunk(x_sample, layers, consts, pasts, ck, cv, page_table)
    return (y_p, y_s, k_p, v_p, k_s, v_s, ssm_p, ssm_s, conv_p, conv_s, wkv_p, wkv_s, shift_p, shift_s, sgu_v_s)
```

```python
import functools

import jax
import jax.numpy as jnp
from jax import lax
from jax.experimental import pallas as pl
from jax.experimental.pallas import tpu as pltpu

F32, BF16, I32 = jnp.float32, jnp.bfloat16, jnp.int32

D_MODEL = 1024
HEAD_DIM = 64
HEADS = 4
GW = HEADS * HEAD_DIM
RMS_EPS = 1e-6
LOG2E = 1.4426950408889634
PAGE = 128
SSM_STATE = 128
SSM_GROUPS = 2
SSM_CONV = 4
SSM_CONV_DIM = GW + 2 * SSM_GROUPS * SSM_STATE
CHUNK = 128
RWKV_PROJ = 896
RWKV_LORA_LO = 3 * GW
RWKV_LORA_W = RWKV_PROJ - RWKV_LORA_LO
DECAY_LORA, ICLR_LORA, GATE_LORA = 32, 32, 64
RWKV_GN_EPS = HEAD_DIM * 1e-5
N_EXPERTS = 8
MOE_FF_BLOCKS = 2
LANE = 128
SUBLANE = 8
VMEM_LIMIT = 52 * 1024 * 1024
ATTN_TQ, ATTN_TK = 512, 256

_P_WIDTHS = (GW, GW, GW, GW, SSM_CONV_DIM, LANE, RWKV_PROJ, 2 * GW)
_P_TOTAL = sum(_P_WIDTHS)


def _cparams(sem):
    return pltpu.CompilerParams(dimension_semantics=sem, vmem_limit_bytes=VMEM_LIMIT)


def _const_spec(shape):
    nd = len(shape)
    return pl.BlockSpec(shape, lambda *_: (0,) * nd)


def _mm(a, b):
    return jnp.dot(a.astype(BF16), b.astype(BF16), preferred_element_type=F32)


def _mm_nt(a, b):
    return lax.dot_general(a.astype(BF16), b.astype(BF16), (((1,), (1,)), ((), ())),
                           preferred_element_type=F32)


def _parts(x, n):
    out, r = [], x
    for i in range(n):
        p = r.astype(BF16)
        out.append(p)
        if i + 1 < n:
            r = r - p.astype(F32)
    return out


def _mm_xl(a, b_exact, n):
    acc = None
    for p in _parts(a, n):
        t = jnp.dot(p, b_exact, preferred_element_type=F32)
        acc = t if acc is None else acc + t
    return acc


def _mm_xr(a_exact, b, n):
    acc = None
    for p in _parts(b, n):
        t = jnp.dot(a_exact, p, preferred_element_type=F32)
        acc = t if acc is None else acc + t
    return acc


def _seg_sum(x, bd):
    return _mm_xl(x, bd, 3)


def _sigmoid(x):
    return 1.0 / (1.0 + jnp.exp(-x))


def _silu(x):
    return x * _sigmoid(x)


def _softplus(x):
    return jnp.maximum(x, 0.0) + jnp.log1p(jnp.exp(-jnp.abs(x)))


def _gelu_tanh(x):
    return 0.5 * x * (1.0 + jnp.tanh(0.7978845608028654 * (x + 0.044715 * (x * x * x))))


def _rms(x, gain):
    return x * lax.rsqrt(jnp.mean(x * x, axis=-1, keepdims=True) + RMS_EPS) * gain


def _proj_kernel(x_ref, g_ref, w_ref, wdt_ref, bd_ref, gq_ref, gk_ref,
                 q_out, k_out, v_out, z_out, xbc_out, dt_out, c_out, d_out, *, attn_layout):
    h32 = _rms(x_ref[...], g_ref[...])
    h = h32.astype(BF16)
    bd = bd_ref[...]
    outs = (q_out, k_out, v_out, z_out, xbc_out, dt_out, c_out, d_out)
    lo = 0
    for idx, (width, out) in enumerate(zip(_P_WIDTHS, outs)):
        if out is dt_out:
            h_lo = (h32 - h.astype(F32)).astype(BF16)
            w_hi, w_lo = _parts(wdt_ref[...], 2)
            mm = lambda a, b: jnp.dot(a, b, preferred_element_type=F32)
            y = mm(h, w_hi) + (mm(h, w_lo) + mm(h_lo, w_hi))
        else:
            y = jnp.dot(h, w_ref[:, lo:lo + width], preferred_element_type=F32)
        if idx < 2:
            gain = (gq_ref, gk_ref)[idx][...]
            y = y * lax.rsqrt(_seg_sum(y * y, bd) * (1.0 / HEAD_DIM) + RMS_EPS) * gain
        if attn_layout and out is q_out:
            for hd in range(HEADS):
                q_out[0, hd] = y[:, hd * HEAD_DIM:(hd + 1) * HEAD_DIM]
        elif attn_layout and (out is k_out or out is v_out):
            out[0] = y.T
        else:
            out[...] = y
        lo += width


def _proj(x2, g, w, w_dt, bd, gq, gk, tm, seq, attn_layout):
    n = x2.shape[0]
    row = lambda wd: pl.BlockSpec((tm, wd), lambda i: (i, 0))
    out_specs = [row(wd) for wd in _P_WIDTHS]
    out_shape = [jax.ShapeDtypeStruct((n, wd), F32) for wd in _P_WIDTHS]
    if attn_layout:
        nb, per = n // seq, seq // tm
        out_specs[0] = pl.BlockSpec((1, HEADS, tm, HEAD_DIM), lambda i: (i // per, 0, i % per, 0))
        out_shape[0] = jax.ShapeDtypeStruct((nb, HEADS, seq, HEAD_DIM), F32)
        for j in (1, 2):
            out_specs[j] = pl.BlockSpec((1, GW, tm), lambda i: (i // per, 0, i % per))
            out_shape[j] = jax.ShapeDtypeStruct((nb, GW, seq), F32)
    return pl.pallas_call(
        functools.partial(_proj_kernel, attn_layout=attn_layout),
        grid=(n // tm,),
        in_specs=[row(D_MODEL),
                  _const_spec((1, D_MODEL)), _const_spec((D_MODEL, _P_TOTAL)), _const_spec((D_MODEL, LANE)),
                  _const_spec((GW, GW)), _const_spec((1, GW)), _const_spec((1, GW))],
        out_specs=out_specs,
        out_shape=out_shape,
        compiler_params=_cparams(("parallel",)),
        name="proj",
    )(x2, g, w, w_dt, bd, gq, gk)


def _sb_weights(neg_zs, readables, u, carry, cumsum_pieces):
    state = _sb_logs(neg_zs, readables, u, carry, cumsum_pieces)
    return _sb_finish(readables, *state)


def _sb_logs(neg_zs, readables, u, carry, cumsum_pieces):
    lks, lbs = [], []
    for nz, readable in zip(neg_zs, readables):
        lk = jnp.minimum(nz, 0.0) - jnp.log2(1.0 + jnp.exp2(-jnp.abs(nz)))
        lbs.append(lk - nz)
        lks.append(lk if readable is None else jnp.where(readable, lk, 0.0))
    if lks[0].shape[0] < LANE:
        rows = lks[0].shape[0]
        stacked = _mm_xl(jnp.concatenate(lks, axis=0), u, cumsum_pieces)
        tails = [stacked[c * rows:(c + 1) * rows] for c in range(len(lks))]
    else:
        tails = [_mm_xl(lk, u, cumsum_pieces) for lk in lks]
    scales = []
    for lk in lks:
        scales.append(jnp.exp2(carry))
        carry = carry + jnp.sum(lk, axis=1, keepdims=True)
    return lbs, tails, scales, carry


def _sb_finish(readables, lbs, tails, scales, carry):
    ws = []
    for readable, lb, tail in zip(readables, lbs, tails):
        w = jnp.exp2(lb + tail)
        ws.append(w if readable is None else jnp.where(readable, w, 0.0))
    return ws, scales, carry


def _attn_prompt_kernel(bias_ref, q_ref, k_ref, v_ref, u_ref, gain_ref, o_ref, acc_ref, cs_ref, nz_ref, *, tq, tk):
    h = pl.program_id(1)
    i = pl.program_id(2)
    neg_bias = bias_ref[h]
    u = u_ref[...]
    acc_ref[...] = jnp.zeros_like(acc_ref)
    cs_ref[...] = jnp.zeros_like(cs_ref)
    cg = tq // tk
    key_minus_query = lax.broadcasted_iota(I32, (tq, tk), 1) - lax.broadcasted_iota(I32, (tq, tk), 0)
    q = q_ref[0, 0].astype(BF16)

    def chunk_start(g, d):
        return pl.multiple_of(jnp.maximum(g, 0) * tq + (cg - 1 - d) * tk, tk)

    def logits(g):
        return [jnp.dot(q, k_ref[0, :, pl.ds(chunk_start(g, d), tk)].astype(BF16), preferred_element_type=F32)
                + neg_bias for d in range(cg)]

    def group(g, masked, nzs, acc, carry):
        starts = [chunk_start(g, d) for d in range(cg)]
        readables = [(key_minus_query < i * tq - ks) if masked else None for ks in starts]
        state = _sb_logs(nzs, readables, u, carry, 1)
        next_nzs = logits(g - 1)
        ws, scales, carry = _sb_finish(readables, *state)
        pvs = [lax.dot_general(w.astype(BF16), v_ref[0, :, pl.ds(ks, tk)].astype(BF16), (((1,), (1,)), ((), ())),
                               preferred_element_type=F32) for w, ks in zip(ws, starts)]
        for sc, pv in zip(scales, pvs):
            acc = acc + sc * pv
        return next_nzs, acc, carry

    def load_state():
        return [nz_ref[d] for d in range(cg)], acc_ref[...], cs_ref[...]

    def store_state(nzs, acc, carry):
        for d in range(cg):
            nz_ref[d] = nzs[d]
        acc_ref[...] = acc
        cs_ref[...] = carry

    store_state(*group(i, True, logits(i), jnp.zeros(acc_ref.shape, F32), jnp.zeros(cs_ref.shape, F32)))

    def past_pair(j, carry):
        g = i - 1 - 2 * j
        store_state(*group(g - 1, False, *group(g, False, *load_state())))
        return carry

    lax.fori_loop(0, i // 2, past_pair, 0)

    @pl.when(i % 2 == 1)
    def _():
        store_state(*group(0, False, *load_state()))

    o_ref[0, 0] = _rms(acc_ref[...], gain_ref[0])


def _attn_prompt(q, k, v, bias, gain, tq, tk):
    b, h, l, d = q.shape
    u = (jnp.arange(tk)[:, None] > jnp.arange(tk)[None, :]).astype(BF16)
    kern = functools.partial(_attn_prompt_kernel, tq=tq, tk=tk)
    return pl.pallas_call(
        kern,
        grid=(b, h, l // tq),
        in_specs=[pl.BlockSpec(memory_space=pltpu.SMEM),
                  pl.BlockSpec((1, 1, tq, d), lambda bi, hi, qi: (bi, hi, qi, 0)),
                  pl.BlockSpec((1, d, l), lambda bi, hi, qi: (bi, hi, 0)),
                  pl.BlockSpec((1, d, l), lambda bi, hi, qi: (bi, hi, 0)),
                  _const_spec((tk, tk)),
                  pl.BlockSpec((1, 1, d), lambda bi, hi, qi: (hi, 0, 0))],
        out_specs=pl.BlockSpec((1, 1, tq, d), lambda bi, hi, qi: (bi, hi, qi, 0)),
        out_shape=jax.ShapeDtypeStruct((b, h, l, d), F32),
        scratch_shapes=[pltpu.VMEM((tq, d), F32), pltpu.VMEM((tq, 1), F32), pltpu.VMEM((tq // tk, tq, tk), F32)],
        compiler_params=_cparams(("parallel", "parallel", "arbitrary")),
        name="attn_prompt",
    )(bias, q, k, v, u, gain)


def _attn_sample_kernel(pt_ref, q_ref, bias_ref, *refs, n_tok, n_group):
    kp_refs, vp_refs = refs[:n_group], refs[n_group:2 * n_group]
    kn_ref, vn_ref, u_ref, hm_ref, gain_ref, o_ref, acc_ref, cs_ref = refs[2 * n_group:]
    p = pl.program_id(1)
    rows = HEADS * n_tok
    q = q_ref[0].astype(BF16)

    def process(ks, vs, readables, acc, carry):
        neg_zs = [_mm(q, k_t) + bias_ref[...] for k_t in ks]
        ws, scales, carry = _sb_weights(neg_zs, readables, u_ref[...], carry, 1)
        for w, sc, v_t in zip(ws, scales, vs):
            acc = acc + sc * _mm_nt(w, v_t)
        return acc, carry

    @pl.when(p == 0)
    def _():
        tok = lax.rem(lax.broadcasted_iota(I32, (rows, PAGE), 0), n_tok)
        key = lax.broadcasted_iota(I32, (rows, PAGE), 1)
        acc, carry = process([kn_ref[0]], [vn_ref[0]], [key < tok],
                             jnp.zeros(acc_ref.shape, F32), jnp.zeros(cs_ref.shape, F32))
        acc_ref[...] = acc
        cs_ref[...] = carry

    order = list(reversed(range(n_group)))
    acc, carry = process([kp_refs[j][...] for j in order], [vp_refs[j][...] for j in order], [None] * n_group,
                         acc_ref[...], cs_ref[...])
    acc_ref[...] = acc
    cs_ref[...] = carry

    @pl.when(p == pl.num_programs(1) - 1)
    def _():
        o = acc_ref[...] * hm_ref[...]
        ss = jnp.sum(o * o, axis=1, keepdims=True) * (1.0 / HEAD_DIM)
        o_ref[0] = o * lax.rsqrt(ss + RMS_EPS) * gain_ref[...]


def _attn_sample(q16, bias16, cache_k, cache_v, layer, page_table, k_new, v_new, hm16, gain):
    s, rows, _ = q16.shape
    n_pages = page_table.shape[1]
    u = (jnp.arange(PAGE)[:, None] > jnp.arange(PAGE)[None, :]).astype(BF16)

    n_group = next(g for g in (32, 16, 8, 4, 2, 1) if n_pages % g == 0)

    def page_spec(j):
        return pl.BlockSpec((None, None, GW, PAGE),
                            lambda si, p, pt: (layer, pt[si, n_pages - (p + 1) * n_group + j], 0, 0))

    page_specs = [page_spec(j) for j in range(n_group)]
    seq_spec = lambda shape: pl.BlockSpec(shape, lambda si, p, pt: (si, 0, 0))
    cst = lambda shape: pl.BlockSpec(shape, lambda si, p, pt: (0,) * len(shape))
    kern = functools.partial(_attn_sample_kernel, n_tok=rows // HEADS, n_group=n_group)
    return pl.pallas_call(
        kern,
        grid_spec=pltpu.PrefetchScalarGridSpec(
            num_scalar_prefetch=1,
            grid=(s, n_pages // n_group),
            in_specs=[seq_spec((1, rows, GW)), cst((rows, 1))] + page_specs + page_specs
            + [seq_spec((1, GW, PAGE)), seq_spec((1, GW, PAGE)),
               cst((PAGE, PAGE)), cst((rows, GW)), cst((1, GW))],
            out_specs=seq_spec((1, rows, GW)),
            scratch_shapes=[pltpu.VMEM((rows, GW), F32), pltpu.VMEM((rows, 1), F32)]),
        out_shape=jax.ShapeDtypeStruct((s, rows, GW), F32),
        compiler_params=_cparams(("parallel", "arbitrary")),
        name="attn_sample",
    )(page_table, q16, bias16, *([cache_k] * n_group), *([cache_v] * n_group), k_new, v_new, u, hm16, gain)


def _ssd_kernel(z_ref, x_ref, dt_ref, c0_ref, s0_ref, cw_ref, cb_ref, dtb_ref, a_ref, dsk_ref, nrm_ref, lt_ref,
                y_ref, st_out, xf_ref, st_ref, *, n_valid):
    t = CHUNK
    c = pl.program_id(1)

    @pl.when(c == 0)
    def _():
        xf_ref[0:SUBLANE, :] = c0_ref[0]
        st_ref[...] = s0_ref[0]

    xf_ref[SUBLANE:SUBLANE + t, :] = x_ref[0]
    conv = cb_ref[...]
    for i in range(SSM_CONV):
        conv = conv + cw_ref[i:i + 1, :] * xf_ref[pl.ds(SUBLANE - (SSM_CONV - 1) + i, t), :]
    xf_ref[0:SUBLANE, :] = xf_ref[t:t + SUBLANE, :]
    xc = _silu(conv)
    xs = xc[:, :GW]
    bm = xc[:, GW:GW + SSM_GROUPS * SSM_STATE]
    cm = xc[:, GW + SSM_GROUPS * SSM_STATE:]

    dt = _softplus(dt_ref[0] + dtb_ref[...])
    if n_valid < t:
        dt = jnp.where(lax.broadcasted_iota(I32, dt.shape, 0) < n_valid, dt, 0.0)
    acum = _mm_xr(lt_ref[...], dt * a_ref[...], 3)
    acum_t = acum.T
    dt_t = dt.T
    a_last = acum[t - 1:t, :]

    row = lax.broadcasted_iota(I32, (t, t), 0)
    col = lax.broadcasted_iota(I32, (t, t), 1)
    causal = row >= col
    lane = lax.broadcasted_iota(I32, (1, GW), 1)
    gmat = [_mm_nt(cm[:, g * SSM_STATE:(g + 1) * SSM_STATE], bm[:, g * SSM_STATE:(g + 1) * SSM_STATE])
            for g in range(SSM_GROUPS)]
    st_prev = st_ref[...]
    y = jnp.zeros((t, GW), F32)
    st_new = jnp.zeros((SSM_STATE, GW), F32)
    e_acum = jnp.zeros((t, GW), F32)
    a_end = jnp.zeros((1, GW), F32)
    for h in range(HEADS):
        g = h // (HEADS // SSM_GROUPS)
        mh = (lane // HEAD_DIM == h).astype(F32)
        col_h = acum[:, h:h + 1]
        decay = jnp.exp(jnp.where(causal, col_h - acum_t[h:h + 1, :], -1e30))
        scores = gmat[g] * decay * dt_t[h:h + 1, :]
        xm = xs * mh
        y = y + _mm(scores, xm)
        to_end = jnp.exp(a_last[:, h:h + 1] - col_h) * dt[:, h:h + 1]
        bw = bm[:, g * SSM_STATE:(g + 1) * SSM_STATE] * to_end
        st_new = st_new + _mm(bw.T, xm)
        e_acum = e_acum + mh * jnp.exp(col_h)
        a_end = a_end + mh * a_last[:, h:h + 1]
    y_in = jnp.where(lane < GW // SSM_GROUPS, _mm(cm[:, :SSM_STATE], st_prev), _mm(cm[:, SSM_STATE:], st_prev))
    st = st_prev * jnp.exp(a_end) + st_new
    st_ref[...] = st
    st_out[0] = st
    y = y + y_in * e_acum + dsk_ref[...] * xs
    y = y * _silu(z_ref[0])
    y_ref[0] = _rms(y, nrm_ref[...])


def _ssd(z, xbc, dt, conv0, st0, cw, cb, dtb, a_neg, dsk, nrm, n_valid):
    b, l, _ = z.shape
    lt = (jnp.arange(CHUNK)[:, None] >= jnp.arange(CHUNK)[None, :]).astype(BF16)
    blk = lambda wd: pl.BlockSpec((1, CHUNK, wd), lambda bi, ci: (bi, ci, 0))
    per_b = lambda shape: pl.BlockSpec(shape, lambda bi, ci: (bi, 0, 0))
    kern = functools.partial(_ssd_kernel, n_valid=n_valid)
    return pl.pallas_call(
        kern,
        grid=(b, l // CHUNK),
        in_specs=[blk(GW), blk(SSM_CONV_DIM), blk(LANE),
                  per_b((1, SUBLANE, SSM_CONV_DIM)), per_b((1, SSM_STATE, GW)),
                  _const_spec((SSM_CONV, SSM_CONV_DIM)), _const_spec((1, SSM_CONV_DIM)),
                  _const_spec((1, LANE)), _const_spec((1, LANE)), _const_spec((1, GW)), _const_spec((1, GW)),
                  _const_spec((CHUNK, CHUNK))],
        out_specs=[blk(GW), per_b((1, SSM_STATE, GW))],
        out_shape=[jax.ShapeDtypeStruct((b, l, GW), F32), jax.ShapeDtypeStruct((b, SSM_STATE, GW), F32)],
        scratch_shapes=[pltpu.VMEM((CHUNK + SUBLANE, SSM_CONV_DIM), F32), pltpu.VMEM((SSM_STATE, GW), F32)],
        compiler_params=_cparams(("parallel", "arbitrary")),
        name="ssd",
    )(z, xbc, dt, conv0, st0, cw, cb, dtb, a_neg, dsk, nrm, lt)


def _rwkv_kernel(p_ref, sh0_ref, s0_ref, mu_ref, w0_ref, a0_ref, dup_ref, iup_ref, gup_ref,
                 kk_ref, ka_ref, rk_ref, gnw_ref, gnb_ref, bd_ref, lt_ref,
                 y_ref, s_out, pf_ref, st_ref, *, bb, t, n_sub, n_valid):
    c = pl.program_id(1)
    bd = bd_ref[...]
    lane = lax.broadcasted_iota(I32, (1, GW), 1)
    head_mask = [(lane // HEAD_DIM == h).astype(F32) for h in range(HEADS)]
    stack = lambda x: jnp.concatenate([x * m for m in head_mask], axis=0)
    unstack = lambda x: sum(x[h * t:(h + 1) * t] for h in range(HEADS))
    step_r = lax.broadcasted_iota(I32, (HEADS * t, HEADS * t), 0) % t
    step_c = lax.broadcasted_iota(I32, (HEADS * t, HEADS * t), 1) % t
    nt = lambda x, y: lax.dot_general(x, y, (((1,), (1,)), ((), ())), preferred_element_type=F32)
    tn = lambda x, y: lax.dot_general(x, y, (((0,), (0,)), ((), ())), preferred_element_type=F32)
    mm = lambda x, y: jnp.dot(x, y, preferred_element_type=F32)

    @pl.when(c == 0)
    def _():
        for b in range(bb):
            pf_ref[b, 0:SUBLANE, :] = sh0_ref[b]
            st_ref[b] = stack(s0_ref[b])

    tt = n_sub * t
    subs = [[] for _ in range(n_sub)]
    for b in range(bb):
        pf_ref[b, SUBLANE:SUBLANE + tt, :] = p_ref[b]
        cur = p_ref[b]
        prev = pf_ref[b, pl.ds(SUBLANE - 1, tt), :]
        pf_ref[b, 0:SUBLANE, :] = pf_ref[b, tt:tt + SUBLANE, :]
        xs = cur + (prev - cur) * mu_ref[...]
        r = xs[:, 0:GW]
        k = xs[:, GW:2 * GW]
        v = xs[:, 2 * GW:3 * GW]
        lo = xs[:, RWKV_LORA_LO:RWKV_PROJ]
        w_log = -_softplus(-(w0_ref[...] + _mm(jnp.tanh(lo), dup_ref[...]))) - 0.5
        a = _sigmoid(a0_ref[...] + _mm(lo, iup_ref[...]))
        kkr = k * kk_ref[...]
        kk = kkr / jnp.maximum(jnp.sqrt(_seg_sum(kkr * kkr, bd)), 1e-12)
        kmod = k * (1.0 + (a - 1.0) * ka_ref[...])
        ka = kk * a
        log_w = -jnp.exp(w_log)
        if n_valid is not None:
            valid = c * tt + lax.broadcasted_iota(I32, (tt, GW), 0) < n_valid
            log_w = jnp.where(valid, log_w, 0.0)
            kk, ka, kmod, v = (jnp.where(valid, x, 0.0) for x in (kk, ka, kmod, v))
        bonus = _seg_sum(r * kmod * rk_ref[...], bd) * v
        gate = _mm(_sigmoid(lo), gup_ref[...])

        for sub in range(n_sub):
            rows = slice(sub * t, (sub + 1) * t)
            lw = log_w[rows]
            cum = _mm_xr(lt_ref[...], lw, 3)
            inv_g = jnp.exp(-cum)
            subs[sub].append(dict(
                a=stack(kk[rows] * jnp.exp(cum - lw)).astype(BF16), b=stack(ka[rows] * inv_g).astype(BF16),
                k=stack(kmod[rows] * inv_g).astype(BF16), r=stack(r[rows] * jnp.exp(cum)).astype(BF16),
                v=stack(v[rows]).astype(BF16), g_end=jnp.exp(cum[t - 1:t, :]),
                bonus=bonus[rows], gate=gate[rows]))

    strict = step_r > step_c
    incl = step_r >= step_c
    states = [st_ref[b] for b in range(bb)]
    for sub, seqs in enumerate(subs):
        for s, s0 in zip(seqs, states):
            s["s0"] = s0
            s["s0_b"] = s0.astype(BF16)
            s["power"] = jnp.where(strict, -nt(s["a"], s["b"]), 0.0).astype(BF16)
            s["l_ak"] = jnp.where(strict, nt(s["a"], s["k"]), 0.0).astype(BF16)
        for s in seqs:
            s["u"] = nt(s["a"], s["s0_b"]) + mm(s["l_ak"], s["v"])
        for s in seqs:
            s["l_rk"] = jnp.where(incl, nt(s["r"], s["k"]), 0.0).astype(BF16)
            s["l_rb"] = jnp.where(incl, nt(s["r"], s["b"]), 0.0).astype(BF16)
        for s in seqs:
            s["u"] = s["u"] + mm(s["power"], s["u"].astype(BF16))
        for _ in range(t.bit_length() - 2):
            for s in seqs:
                s["power"] = mm(s["power"], s["power"]).astype(BF16)
            for s in seqs:
                s["u"] = s["u"] + mm(s["power"], s["u"].astype(BF16))
        for s in seqs:
            s["u_b"] = s["u"].astype(BF16)
            s["y"] = unstack(nt(s["r"], s["s0_b"]) + mm(s["l_rk"], s["v"]) - mm(s["l_rb"], s["u_b"]))
        states = [(s["s0"] + tn(s["v"], s["k"]) - tn(s["u_b"], s["b"])) * s["g_end"] for s in seqs]

        for b, s in enumerate(seqs):
            y = s["y"]
            mean = _seg_sum(y, bd) * (1.0 / HEAD_DIM)
            yc = y - mean
            var = _seg_sum(yc * yc, bd) * (1.0 / HEAD_DIM)
            yn = yc * lax.rsqrt(var + RWKV_GN_EPS) * gnw_ref[...] + gnb_ref[...]
            y_ref[b, sub * t:(sub + 1) * t, :] = (yn + s["bonus"]) * s["gate"]

    for b in range(bb):
        st_ref[b] = states[b]
        s_out[b] = unstack(states[b])


def _rwkv(pc, sh0, s0, mu, w0, a0, dup, iup, gup, k_k, k_a, r_k, gnw, gnb, bd, bb, n_valid):
    b, l, _ = pc.shape
    t = HEAD_DIM
    lt = (jnp.arange(t)[:, None] >= jnp.arange(t)[None, :]).astype(BF16)
    n_sub = 2 if l % (2 * t) == 0 else 1
    tt = n_sub * t
    blk = lambda wd: pl.BlockSpec((bb, tt, wd), lambda bi, ci: (bi, ci, 0))
    per_b = lambda shape: pl.BlockSpec(shape, lambda bi, ci: (bi, 0, 0))
    vec = lambda wd: _const_spec((1, wd))
    kern = functools.partial(_rwkv_kernel, bb=bb, t=t, n_sub=n_sub, n_valid=n_valid if n_valid < l else None)
    return pl.pallas_call(
        kern,
        grid=(b // bb, l // tt),
        in_specs=[blk(RWKV_PROJ), per_b((bb, SUBLANE, RWKV_PROJ)), per_b((bb, HEAD_DIM, GW)),
                  vec(RWKV_PROJ), vec(GW), vec(GW),
                  _const_spec((RWKV_LORA_W, GW)), _const_spec((RWKV_LORA_W, GW)), _const_spec((RWKV_LORA_W, GW)),
                  vec(GW), vec(GW), vec(GW), vec(GW), vec(GW),
                  _const_spec((GW, GW)), _const_spec((t, t))],
        out_specs=[blk(GW), per_b((bb, HEAD_DIM, GW))],
        out_shape=[jax.ShapeDtypeStruct((b, l, GW), F32), jax.ShapeDtypeStruct((b, HEAD_DIM, GW), F32)],
        scratch_shapes=[pltpu.VMEM((bb, tt + SUBLANE, RWKV_PROJ), F32), pltpu.VMEM((bb, HEADS * HEAD_DIM, GW), F32)],
        compiler_params=_cparams(("parallel", "arbitrary")),
        name="rwkv",
    )(pc, sh0, s0, mu, w0, a0, dup, iup, gup, k_k, k_a, r_k, gnw, gnb, bd, lt)


def _sgu_kernel(p_ref, w_ref, bias_ref, gv_ref, bd_ref, o_ref, v_out, *, n_sub):
    t = CHUNK
    p = p_ref[0]
    u = _gelu_tanh(p[:, :GW])
    v = _gelu_tanh(p[:, GW:])
    vn = v * lax.rsqrt(_seg_sum(v * v, bd_ref[...]) * (1.0 / HEAD_DIM) + RMS_EPS) * gv_ref[...]
    v_out[0] = vn
    causal = lax.broadcasted_iota(I32, (t, t), 0) >= lax.broadcasted_iota(I32, (t, t), 1)
    lane = lax.broadcasted_iota(I32, (1, GW), 1)
    w_tril = [jnp.where(causal, w_ref[g], 0.0).astype(BF16) for g in range(HEADS)]
    head = [(lane // HEAD_DIM == g).astype(F32) for g in range(HEADS)]
    for s in range(n_sub):
        rows = slice(s * t, (s + 1) * t)
        mixed = bias_ref[...]
        for g in range(HEADS):
            mixed = mixed + _mm(w_tril[g], vn[rows] * head[g])
        o_ref[0, rows, :] = u[rows] * mixed


def _sgu(pd, w, bias, gv, bd):
    b, l, _ = pd.shape
    n_sub = next(s for s in (4, 2, 1) if l % (s * CHUNK) == 0)
    rows = n_sub * CHUNK
    return pl.pallas_call(
        functools.partial(_sgu_kernel, n_sub=n_sub),
        grid=(b, l // rows),
        in_specs=[pl.BlockSpec((1, rows, 2 * GW), lambda bi, ci: (bi, ci, 0)),
                  _const_spec((HEADS, CHUNK, CHUNK)), _const_spec((CHUNK, GW)), _const_spec((1, GW)),
                  _const_spec((GW, GW))],
        out_specs=[pl.BlockSpec((1, rows, GW), lambda bi, ci: (bi, ci, 0))] * 2,
        out_shape=[jax.ShapeDtypeStruct((b, l, GW), F32)] * 2,
        compiler_params=_cparams(("parallel", "parallel")),
        name="sgu",
    )(pd, w, bias, gv, bd)


def _outproj_kernel(x_ref, oa_ref, ob_ref, oc_ref, od_ref, w_ref, o_ref, *, oa_heads):
    o_ref[...] = _mix_residual(x_ref, oa_ref, ob_ref, oc_ref, od_ref, w_ref, oa_heads)


def _mix_residual(x_ref, oa_ref, ob_ref, oc_ref, od_ref, w_ref, oa_heads):
    acc = x_ref[...]
    if oa_heads:
        for hd in range(HEADS):
            acc = acc + jnp.dot(oa_ref[0, hd].astype(BF16), w_ref[hd * HEAD_DIM:(hd + 1) * HEAD_DIM, :],
                                preferred_element_type=F32)
    for i, r in enumerate((oa_ref, ob_ref, oc_ref, od_ref)):
        if i > 0 or not oa_heads:
            acc = acc + jnp.dot(r[...].astype(BF16), w_ref[i * GW:(i + 1) * GW, :], preferred_element_type=F32)
    return acc


def _mix_specs(oa, tm, index):
    row = lambda wd: pl.BlockSpec((tm, wd), lambda *ids: (index(*ids), 0))
    if oa.ndim == 4:
        per = oa.shape[2] // tm
        oa_spec = pl.BlockSpec((1, HEADS, tm, HEAD_DIM), lambda *ids: (index(*ids) // per, 0, index(*ids) % per, 0))
    else:
        oa_spec = row(GW)
    return [row(D_MODEL), oa_spec, row(GW), row(GW), row(GW), _const_spec((4 * GW, D_MODEL))]


def _outproj(x2, oa, ob, oc, od, w, tm):
    n = x2.shape[0]
    row = lambda wd: pl.BlockSpec((tm, wd), lambda i: (i, 0))
    return pl.pallas_call(
        functools.partial(_outproj_kernel, oa_heads=oa.ndim == 4),
        grid=(n // tm,),
        in_specs=_mix_specs(oa, tm, lambda i: i),
        out_specs=row(D_MODEL),
        out_shape=jax.ShapeDtypeStruct((n, D_MODEL), F32),
        compiler_params=_cparams(("parallel",)),
        name="outproj",
    )(x2, oa, ob, oc, od, w)


def _ffn_kernel(x_ref, oa_ref, ob_ref, oc_ref, od_ref, wo_ref, g_ref, wg_ref, wu_ref, wd_ref, o_ref, h_s, *, oa_heads):
    f = pl.program_id(1)

    @pl.when(f == 0)
    def _():
        x_mid = _mix_residual(x_ref, oa_ref, ob_ref, oc_ref, od_ref, wo_ref, oa_heads)
        h_s[...] = _rms(x_mid, g_ref[...]).astype(BF16)
        o_ref[...] = x_mid

    h = h_s[...]
    a = jnp.dot(h, wg_ref[...], preferred_element_type=F32)
    u = jnp.dot(h, wu_ref[...], preferred_element_type=F32)
    o_ref[...] += jnp.dot((_silu(a) * u).astype(BF16), wd_ref[...], preferred_element_type=F32)


def _ffn(x2, oa, ob, oc, od, w_out, g, wg, wu, wd, tm, tf):
    n = x2.shape[0]
    d_ff = wg.shape[1]
    return pl.pallas_call(
        functools.partial(_ffn_kernel, oa_heads=oa.ndim == 4),
        grid=(n // tm, d_ff // tf),
        in_specs=_mix_specs(oa, tm, lambda i, f: i)
        + [_const_spec((1, D_MODEL)),
           pl.BlockSpec((D_MODEL, tf), lambda i, f: (0, f)),
           pl.BlockSpec((D_MODEL, tf), lambda i, f: (0, f)),
           pl.BlockSpec((tf, D_MODEL), lambda i, f: (f, 0))],
        out_specs=pl.BlockSpec((tm, D_MODEL), lambda i, f: (i, 0)),
        out_shape=jax.ShapeDtypeStruct((n, D_MODEL), F32),
        scratch_shapes=[pltpu.VMEM((tm, D_MODEL), BF16)],
        compiler_params=_cparams(("parallel", "arbitrary")),
        name="ffn",
    )(x2, oa, ob, oc, od, w_out, g, wg, wu, wd)


def _router_kernel(x_ref, g_ref, rwt_ref, idx_ref, gate_ref):
    h = _rms(x_ref[...], g_ref[...])
    hh, hl = _parts(h, 2)
    wh, wl = _parts(rwt_ref[...], 2)
    nt = lambda a, b: lax.dot_general(a, b, (((1,), (1,)), ((), ())), preferred_element_type=F32)
    logits = nt(wh, hh) + (nt(wh, hl) + nt(wl, hh))
    e_id = lax.broadcasted_iota(I32, logits.shape, 0)
    m1 = jnp.max(logits, axis=0, keepdims=True)
    i1 = jnp.min(jnp.where(logits == m1, e_id, N_EXPERTS), axis=0, keepdims=True)
    rest = jnp.where(e_id == i1, -jnp.inf, logits)
    m2 = jnp.max(rest, axis=0, keepdims=True)
    i2 = jnp.min(jnp.where(rest == m2, e_id, N_EXPERTS), axis=0, keepdims=True)
    e = jnp.exp(m2 - m1)
    g1 = 1.0 / (1.0 + e)
    idx_ref[...] = jnp.where(e_id == 0, i1, jnp.where(e_id == 1, i2, 0))
    gate_ref[...] = jnp.where(e_id == 0, g1, jnp.where(e_id == 1, e * g1, 0.0))


def _router(x2, g, rwt, tm):
    n = x2.shape[0]
    return pl.pallas_call(
        _router_kernel,
        grid=(n // tm,),
        in_specs=[pl.BlockSpec((tm, D_MODEL), lambda i: (i, 0)), _const_spec((1, D_MODEL)),
                  _const_spec((N_EXPERTS, D_MODEL))],
        out_specs=[pl.BlockSpec((N_EXPERTS, tm), lambda i: (0, i))] * 2,
        out_shape=[jax.ShapeDtypeStruct((N_EXPERTS, n), I32), jax.ShapeDtypeStruct((N_EXPERTS, n), F32)],
        compiler_params=_cparams(("parallel",)),
        name="router",
    )(x2, g, rwt)


def _moe_kernel(tok_ref, off_ref, cnt_ref, gts_ref, x_ref, g_ref, wg_ref, wu_ref, wd_ref, o_ref,
                xg_s, og_s, *, gr):
    b = pl.program_id(0)
    e = pl.program_id(1)
    f = pl.program_id(2)
    n = cnt_ref[b * N_EXPERTS + e]
    o0 = off_ref[b * N_EXPERTS + e]

    @pl.when((e == 0) & (f == 0))
    def _():
        o_ref[...] = x_ref[...]

    @pl.when((b == 0) & (e == 0) & (f == 0))
    def _():
        xg_s[...] = jnp.zeros_like(xg_s)

    def row_loop(body, unroll=4):
        def main(i, carry):
            for r in range(unroll):
                body(i * unroll + r)
            return carry

        def tail(i, carry):
            body(i)
            return carry
        lax.fori_loop(0, n // unroll, main, 0)
        lax.fori_loop((n // unroll) * unroll, n, tail, 0)

    @pl.when(f == 0)
    def _():
        def gather(i):
            xg_s[pl.ds(i, 1), :] = x_ref[pl.ds(tok_ref[o0 + i], 1), :]
        row_loop(gather, unroll=8)

    def group(gi, carry):
        r0 = pl.multiple_of(gi * gr, gr)

        @pl.when(f == 0)
        def _():
            xg_s[pl.ds(r0, gr), :] = _rms(xg_s[pl.ds(r0, gr), :], g_ref[...])

        xb = xg_s[pl.ds(r0, gr), :].astype(BF16)
        a = jnp.dot(xb, wg_ref[...], preferred_element_type=F32)
        u = jnp.dot(xb, wu_ref[...], preferred_element_type=F32)
        y = jnp.dot((_silu(a) * u).astype(BF16), wd_ref[...], preferred_element_type=F32)

        @pl.when(f == 0)
        def _():
            og_s[pl.ds(r0, gr), :] = y

        @pl.when(f > 0)
        def _():
            og_s[pl.ds(r0, gr), :] += y
        return carry

    lax.fori_loop(0, (n + gr - 1) // gr, group, 0)

    @pl.when(f == pl.num_programs(2) - 1)
    def _():
        def scatter(i):
            tk = tok_ref[o0 + i]
            o_ref[pl.ds(tk, 1), :] += gts_ref[o0 + i] * og_s[pl.ds(i, 1), :]
        row_loop(scatter, unroll=8)


def _moe(x2, g, tok, off, cnt, gts, wg, wu, wd, tm, gr):
    n = x2.shape[0]
    n_f, tf = MOE_FF_BLOCKS, wg.shape[2] // MOE_FF_BLOCKS
    kern = functools.partial(_moe_kernel, gr=gr)
    return pl.pallas_call(
        kern,
        grid_spec=pltpu.PrefetchScalarGridSpec(
            num_scalar_prefetch=3,
            grid=(n // tm, N_EXPERTS, n_f),
            in_specs=[pl.BlockSpec(memory_space=pltpu.SMEM),
                      pl.BlockSpec((tm, D_MODEL), lambda b, e, f, *_: (b, 0), pipeline_mode=pl.Buffered(1)),
                      pl.BlockSpec((1, D_MODEL), lambda b, e, f, *_: (0, 0)),
                      pl.BlockSpec((None, D_MODEL, tf), lambda b, e, f, *_: (e, 0, f)),
                      pl.BlockSpec((None, D_MODEL, tf), lambda b, e, f, *_: (e, 0, f)),
                      pl.BlockSpec((None, tf, D_MODEL), lambda b, e, f, *_: (e, f, 0))],
            out_specs=pl.BlockSpec((tm, D_MODEL), lambda b, e, f, *_: (b, 0)),
            scratch_shapes=[pltpu.VMEM((-(-tm // gr) * gr, D_MODEL), F32),
                            pltpu.VMEM((-(-tm // gr) * gr, D_MODEL), F32)]),
        out_shape=jax.ShapeDtypeStruct((n, D_MODEL), F32),
        compiler_params=_cparams(("arbitrary", "arbitrary", "arbitrary")),
        name="moe",
    )(tok, off, cnt, gts, x2, g, wg, wu, wd)


def _moe_group_rows(tm):
    return min(tm, -(-(9 * tm // 32) // 32) * 32)


def _moe_plan(idx, gate, tm):
    n = idx.shape[1]
    nb = n // tm
    e = idx[:2].T.reshape(nb, 2 * tm)
    gt = gate[:2].T.reshape(nb, 2 * tm)
    order = jnp.argsort(e, axis=1, stable=True)
    tok = (order // 2).astype(I32)
    gts = jnp.take_along_axis(gt, order, axis=1)
    cnt = jnp.sum(e[:, :, None] == jnp.arange(N_EXPERTS, dtype=I32)[None, None, :], axis=1).astype(I32)
    off = jnp.cumsum(cnt, axis=1) - cnt + (jnp.arange(nb, dtype=I32) * (2 * tm))[:, None]
    return tok.reshape(-1), off.reshape(-1).astype(I32), cnt.reshape(-1), gts.reshape(-1)


def _layer(x, l, p, consts, past, cache_k, cache_v, page_table):
    b, seq, _ = x.shape
    n = b * seq
    is_sample = past is not None
    tm = min(n, 512)
    x2 = x.reshape(n, D_MODEL)
    bd, hm = consts["bd"], consts["hm"]

    qn, kn, v, z, xbc, dtr, pc, pd = _proj(x2, p["norm_mix"], p["w_in"], p["w_dt"], bd, p["gq"], p["gk"], tm,
                                           seq, not is_sample)

    if not is_sample:
        from_feature_major = lambda a: a.reshape(b, HEADS, HEAD_DIM, seq).transpose(0, 3, 1, 2)
        k_new, v_new = from_feature_major(kn), from_feature_major(v)
        oa = _attn_prompt(qn, kn, v, p["sb_bias"], p["sb_out_norm"].reshape(HEADS, 1, HEAD_DIM),
                          min(seq, ATTN_TQ), min(seq, ATTN_TK))
    else:
        k_new = kn.reshape(b, seq, HEADS, HEAD_DIM)
        v_new = v.reshape(b, seq, HEADS, HEAD_DIM)
        rows = HEADS * seq
        q16 = (qn.reshape(b, 1, seq, GW) * hm[None, :, None, :]).reshape(b, rows, GW)
        pad_keys_t = lambda a: jnp.pad(a.reshape(b, seq, GW), ((0, 0), (0, PAGE - seq), (0, 0))).transpose(0, 2, 1)
        o16 = _attn_sample(q16, jnp.repeat(p["sb_bias"], seq).reshape(rows, 1), cache_k, cache_v, l, page_table,
                           pad_keys_t(kn), pad_keys_t(v), jnp.repeat(hm, seq, axis=0),
                           p["sb_out_norm"].reshape(1, GW))
        oa = o16.reshape(b, HEADS, seq, GW).sum(axis=1).reshape(n, GW)

    lc = -(-seq // CHUNK) * CHUNK
    pad_seq = lambda a, to: jnp.pad(a.reshape(b, seq, -1), ((0, 0), (0, to - seq), (0, 0)))

    xbc3 = xbc.reshape(b, seq, SSM_CONV_DIM)
    if is_sample:
        conv0 = past["conv"]
        st0 = past["ssm"].reshape(b, GW, SSM_STATE).transpose(0, 2, 1)
    else:
        conv0 = jnp.zeros((b, SSM_CONV - 1, SSM_CONV_DIM), F32)
        st0 = jnp.zeros((b, SSM_STATE, GW), F32)
    conv8 = jnp.pad(conv0, ((0, 0), (SUBLANE - (SSM_CONV - 1), 0), (0, 0)))
    ob, st = _ssd(pad_seq(z, lc), pad_seq(xbc, lc), pad_seq(dtr, lc), conv8, st0,
                  p["conv_w"], p["conv_b"], p["dt_bias"], p["a_neg"], p["d_skip"], p["ssm_norm"], min(seq, CHUNK))
    ob = ob[:, :seq].reshape(n, GW)
    ssm_new = st.transpose(0, 2, 1).reshape(b, HEADS, HEAD_DIM, SSM_STATE)
    conv_new = jnp.concatenate([conv0, xbc3], axis=1)[:, seq:]

    pc3 = pc.reshape(b, seq, RWKV_PROJ)
    if is_sample:
        shift0 = past["shift"]
        wkv0 = past["wkv"].transpose(0, 2, 1, 3).reshape(b, HEAD_DIM, GW)
    else:
        shift0 = jnp.zeros((b, RWKV_PROJ), F32)
        wkv0 = jnp.zeros((b, HEAD_DIM, GW), F32)
    sh8 = jnp.pad(shift0[:, None, :], ((0, 0), (SUBLANE - 1, 0), (0, 0)))
    lr = -(-seq // HEAD_DIM) * HEAD_DIM
    oc, wkv = _rwkv(pad_seq(pc, lr), sh8, wkv0, p["shift_mu"], p["decay_w0"], p["iclr_a0"],
                    p["decay_up"], p["iclr_up"], p["gate_up"], p["k_k"], p["k_a"], p["r_k"], p["gn_w"], p["gn_b"],
                    bd, 4 if b % 4 == 0 else 2, seq)
    oc = oc[:, :seq].reshape(n, GW)
    wkv_new = wkv.reshape(b, HEAD_DIM, HEADS, HEAD_DIM).transpose(0, 2, 1, 3)
    shift_new = pc3[:, -1]

    od, sgu_v = _sgu(pad_seq(pd, lc), p["sgu_w"], p["sgu_bias"], p["sgu_v_norm"], bd)
    od = od[:, :seq].reshape(n, GW)
    sgu_v = sgu_v[:, :seq].reshape(b, seq, HEADS, HEAD_DIM)

    if "ffn_wg" in p:
        x_out = _ffn(x2, oa, ob, oc, od, p["w_out"], p["norm_ffn"], p["ffn_wg"], p["ffn_wu"], p["ffn_wd"],
                     tm, p["ffn_wg"].shape[1] // 2)
    else:
        x_mid = _outproj(x2, oa, ob, oc, od, p["w_out"], tm)
        tmm = min(n, 1024)
        idx, gate = _router(x_mid, p["norm_ffn"], p["router_wt"], min(n, 512))
        tok, off, cnt, gts = _moe_plan(idx, gate, tmm)
        x_out = _moe(x_mid, p["norm_ffn"], tok, off, cnt, gts, p["moe_wg"], p["moe_wu"], p["moe_wd"],
                     tmm, _moe_group_rows(tmm))
    return x_out.reshape(b, seq, D_MODEL), (k_new, v_new, ssm_new, conv_new, wkv_new, shift_new, sgu_v)


def _trunk(x, layers, consts, pasts, cache_k, cache_v, page_table):
    states = []
    for l, p in enumerate(layers):
        x, st = _layer(x, l, p, consts, None if pasts is None else pasts[l], cache_k, cache_v, page_table)
        states.append(st)
    return x, [jnp.stack(s) for s in zip(*states)]


def _pad_rows(w, lo, total):
    return jnp.pad(w, ((lo, total - lo - w.shape[0]), (0, 0)))


def kernel(x_prompt, x_sample, cache_k, cache_v, page_table, state_ssm, state_conv, state_wkv, state_shift, norm_mix, norm_ffn, w_in, w_out, q_norm, k_norm, sb_bias, sb_out_norm, conv_w, conv_b, dt_bias, a_log, d_skip, ssm_norm, shift_mu, decay_w0, decay_up, iclr_a0, iclr_up, gate_up, k_k, k_a, r_k, gn_w, gn_b, sgu_w, sgu_b, sgu_v_norm, ffn_wg, ffn_wu, ffn_wd, router_w, moe_wg, moe_wu, moe_wd):
    depth = w_in.shape[0]
    lane = jnp.arange(GW)
    consts = {
        "bd": (lane[:, None] // HEAD_DIM == lane[None, :] // HEAD_DIM).astype(BF16),
        "hm": (jnp.arange(HEADS)[:, None] == lane[None, :] // HEAD_DIM).astype(F32),
    }
    dt_col = 4 * GW + SSM_CONV_DIM
    row = lambda a: a.reshape(1, -1).astype(F32)
    layers = []
    for l in range(depth):
        w = w_in[l]
        w_packed = jnp.concatenate(
            [w[:, :dt_col + HEADS], jnp.zeros((D_MODEL, LANE - HEADS), F32), w[:, dt_col + HEADS:]], axis=1)
        p = dict(
            norm_mix=row(norm_mix[l]), norm_ffn=row(norm_ffn[l]),
            w_in=w_packed.astype(BF16), w_dt=w_packed[:, dt_col:dt_col + LANE], w_out=w_out[l].astype(BF16),
            gq=row(jnp.tile(q_norm[l], HEADS)) * -(LOG2E * HEAD_DIM ** -0.5), gk=row(jnp.tile(k_norm[l], HEADS)),
            sb_bias=sb_bias[l].astype(F32) * -LOG2E, sb_out_norm=sb_out_norm[l],
            conv_w=conv_w[l], conv_b=row(conv_b[l]),
            dt_bias=jnp.pad(row(dt_bias[l]), ((0, 0), (0, LANE - HEADS))),
            a_neg=jnp.pad(row(-jnp.exp(a_log[l])), ((0, 0), (0, LANE - HEADS))),
            d_skip=row(jnp.repeat(d_skip[l], HEAD_DIM)), ssm_norm=row(ssm_norm[l]),
            shift_mu=row(shift_mu[l]), decay_w0=row(decay_w0[l]), iclr_a0=row(iclr_a0[l]),
            decay_up=_pad_rows(decay_up[l], 0, RWKV_LORA_W).astype(BF16),
            iclr_up=_pad_rows(iclr_up[l], DECAY_LORA, RWKV_LORA_W).astype(BF16),
            gate_up=_pad_rows(gate_up[l], DECAY_LORA + ICLR_LORA, RWKV_LORA_W).astype(BF16),
            k_k=row(k_k[l]), k_a=row(k_a[l]), r_k=row(r_k[l]), gn_w=row(gn_w[l]), gn_b=row(gn_b[l]),
            sgu_w=sgu_w[l][:, :CHUNK, :CHUNK],
            sgu_bias=jnp.repeat(sgu_b[l][:, :CHUNK].T, HEAD_DIM, axis=1),
            sgu_v_norm=row(sgu_v_norm[l]),
        )
        if l % 2 == 0:
            p.update(ffn_wg=ffn_wg[l // 2].astype(BF16), ffn_wu=ffn_wu[l // 2].astype(BF16),
                     ffn_wd=ffn_wd[l // 2].astype(BF16))
        else:
            p.update(router_wt=router_w[l // 2].T, moe_wg=moe_wg[l // 2].astype(BF16),
                     moe_wu=moe_wu[l // 2].astype(BF16), moe_wd=moe_wd[l // 2].astype(BF16))
        layers.append(p)

    to_pages_t = lambda c: c.transpose(0, 1, 3, 4, 2).reshape(c.shape[0], c.shape[1], GW, PAGE)
    ck, cv = to_pages_t(cache_k), to_pages_t(cache_v)
    pasts = [dict(ssm=state_ssm[l], conv=state_conv[l], wkv=state_wkv[l], shift=state_shift[l])
             for l in range(depth)]

    y_p, (k_p, v_p, ssm_p, conv_p, wkv_p, shift_p, _) = _trunk(x_prompt, layers, consts, None, ck, cv, page_table)
    y_s, (k_s, v_s, ssm_s, conv_s, wkv_s, shift_s, sgu_v_s) = _trunk(x_sample, layers, consts, pasts, ck, cv, page_table)
    return (y_p, y_s, k_p, v_p, k_s, v_s, ssm_p, ssm_s, conv_p, conv_s, wkv_p, wkv_s, shift_p, shift_s, sgu_v_s)
```
